```python
import math
import jax, jax.numpy as jnp
from jax import lax
import numpy as np

D_MODEL = 1024
BATCH = 32
SEQ = 2048
DEPTH = 4

GRID_W = 64
CTX_LEN = 256
N_MOD = 6
ATTN_HEADS = 4
ATTN_QK_DIM = 64
ATTN_V_DIM = 2 * ATTN_QK_DIM
Q_COLS = ATTN_HEADS * 2 * ATTN_QK_DIM
K_COLS = Q_COLS
V_COLS = ATTN_HEADS * ATTN_V_DIM
ATTN_SCALE = ATTN_QK_DIM ** -0.5
ROPE_THETA = 10000.0
BLOCK_Q = 128
CONV_WIDTH = D_MODEL // 4
POOL_WINDOWS = (2, 4, 8, 16)
POOL_GROUPS = len(POOL_WINDOWS)
POOL_WIDTH = D_MODEL - V_COLS - CONV_WIDTH
POOL_GROUP_DIM = POOL_WIDTH // POOL_GROUPS
Q_END = Q_COLS
K_END = Q_END + K_COLS
V_END = K_END + V_COLS
CONV_END = V_END + 3 * CONV_WIDTH
IN_COLS = CONV_END + POOL_WIDTH
MIX_WIDTH = V_COLS + CONV_WIDTH + POOL_WIDTH
FFN_HIDDEN = -(-(8 * D_MODEL) // (3 * 256)) * 256
EPS = 1e-6

kernel_name = "hybrid_diff_conv_pool_dit"


def rms_norm(x, g):
    xf = x.astype(jnp.float32)
    y = xf * lax.rsqrt(jnp.mean(xf * xf, axis=-1, keepdims=True) + EPS)
    return y.astype(x.dtype) * g


def modulate(h, shift, scale):
    return h * (1.0 + scale) + shift


def axial_rope_tables(length, dtype):
    rows = length // GRID_W
    row = jnp.broadcast_to(jnp.arange(rows)[:, None], (rows, GRID_W)).reshape(-1).astype(jnp.float32)
    col = jnp.broadcast_to(jnp.arange(GRID_W)[None, :], (rows, GRID_W)).reshape(-1).astype(jnp.float32)
    nf = ATTN_QK_DIM // 4
    inv = 1.0 / (ROPE_THETA ** (jnp.arange(nf, dtype=jnp.float32) / nf))
    ang_r = row[:, None] * inv[None, :]
    ang_c = col[:, None] * inv[None, :]
    ang = jnp.concatenate([ang_r, ang_r, ang_c, ang_c], axis=-1)
    return jnp.cos(ang).astype(dtype), jnp.sin(ang).astype(dtype)


def apply_rope(t, cos, sin):
    nf = ATTN_QK_DIM // 4
    tr = t.reshape(t.shape[:-1] + (2, 2, nf))
    rot = jnp.stack([-tr[..., 1, :], tr[..., 0, :]], axis=-2).reshape(t.shape)
    return t * cos + rot * sin


def split_heads_qk(t):
    b, l, _ = t.shape
    return t.reshape(b, l, ATTN_HEADS, 2, ATTN_QK_DIM).transpose(0, 2, 3, 1, 4)


def split_heads_v(t):
    b, l, _ = t.shape
    return t.reshape(b, l, ATTN_HEADS, ATTN_V_DIM).transpose(0, 2, 1, 3)


def diff_attention_core(q, k, v, lam):
    s = jnp.einsum('bhcqd,bhckd->bhcqk', q, k).astype(jnp.float32) * ATTN_SCALE
    p = jax.nn.softmax(s, axis=-1)
    a = p[:, :, 0] - lam * p[:, :, 1]
    return jnp.einsum('bhqk,bhkd->bhqd', a.astype(v.dtype), v)


def diff_attn_post(o, g_sub, lam_init):
    o = rms_norm(o, g_sub) * (1.0 - lam_init)
    b, h, l, dv = o.shape
    return o.transpose(0, 2, 1, 3).reshape(b, l, h * dv)


def conv_mixer(bcx, w_conv):
    bg, cg, xin = jnp.split(bcx, 3, axis=-1)
    u = cg * xin
    up = jnp.pad(u, ((0, 0), (1, 1), (0, 0)))
    y = up[:, :-2] * w_conv[0] + up[:, 1:-1] * w_conv[1] + up[:, 2:] * w_conv[2]
    return bg * y


def pool_mixer(p, w_pool, s_pool):
    b, l, _ = p.shape
    pf = p.astype(jnp.float32)
    cs = jnp.concatenate([jnp.zeros((b, 1, POOL_WIDTH), jnp.float32), jnp.cumsum(pf, axis=1)], axis=1)
    cs = cs.reshape(b, l + 1, POOL_GROUPS, POOL_GROUP_DIM)
    t = jnp.arange(l)[:, None]
    win = jnp.array(POOL_WINDOWS, dtype=jnp.int32)[None, :]
    lo = jnp.clip(t - win // 2, 0, l - 1)
    hi = jnp.clip(t + win - 1 - win // 2, 0, l - 1)
    grp = jnp.arange(POOL_GROUPS)[None, :]
    cnt = (hi - lo + 1).astype(jnp.float32)[None, :, :, None]
    mean = (cs[:, hi + 1, grp, :] - cs[:, lo, grp, :]) / cnt
    pooled = (mean - pf.reshape(b, l, POOL_GROUPS, POOL_GROUP_DIM)).astype(p.dtype)
    y = jnp.einsum('blgc,gcd->blgd', pooled, w_pool).reshape(b, l, POOL_WIDTH)
    return y * s_pool


def swiglu(h, w_gate_up, w_down):
    g, u = jnp.split(h @ w_gate_up, 2, axis=-1)
    return (jax.nn.silu(g) * u) @ w_down


def setup_inputs(seed: int = 0) -> dict:
    key = jax.random.key(seed)
    ks = jax.random.split(key, 24)
    f32 = jnp.float32
    D = D_MODEL

    def nrm(k, shape, scale):
        return jax.random.normal(k, shape, f32) * scale

    return {
        'x': nrm(ks[0], (BATCH, SEQ, D), 1.0),
        'c': nrm(ks[1], (BATCH, D), 1.0),
        'ctx': nrm(ks[2], (BATCH, CTX_LEN, D), 1.0),
        'c_ctx': nrm(ks[3], (D,), 1.0),
        'w_ada': nrm(ks[4], (DEPTH, D, N_MOD * D), 0.5 * D ** -0.5),
        'b_ada': nrm(ks[5], (DEPTH, N_MOD * D), 0.02),
        'g_norm1': 1.0 + nrm(ks[6], (DEPTH, D), 0.05),
        'w_in': nrm(ks[7], (DEPTH, D, IN_COLS), D ** -0.5),
        'lam_q1': nrm(ks[8], (DEPTH, ATTN_QK_DIM), 0.1),
        'lam_k1': nrm(ks[9], (DEPTH, ATTN_QK_DIM), 0.1),
        'lam_q2': nrm(ks[10], (DEPTH, ATTN_QK_DIM), 0.1),
        'lam_k2': nrm(ks[11], (DEPTH, ATTN_QK_DIM), 0.1),
        'g_subln': 1.0 + nrm(ks[12], (DEPTH, ATTN_V_DIM), 0.05),
        'w_conv': nrm(ks[13], (DEPTH, 3, CONV_WIDTH), 3 ** -0.5),
        'w_pool': nrm(ks[14], (DEPTH, POOL_GROUPS, POOL_GROUP_DIM, POOL_GROUP_DIM), POOL_GROUP_DIM ** -0.5),
        's_pool': 1.0 + nrm(ks[15], (DEPTH, POOL_WIDTH), 0.1),
        'w_out': nrm(ks[16], (DEPTH, MIX_WIDTH, D), MIX_WIDTH ** -0.5),
        'g_norm2': 1.0 + nrm(ks[17], (DEPTH, D), 0.05),
        'w_gate_up': nrm(ks[18], (DEPTH, D, 2 * FFN_HIDDEN), D ** -0.5),
        'w_down': nrm(ks[19], (DEPTH, FFN_HIDDEN, D), FFN_HIDDEN ** -0.5),
        'g_final': 1.0 + nrm(ks[20], (D,), 0.05),
    }


def reference(x, c, ctx, c_ctx, w_ada, b_ada, g_norm1, w_in, lam_q1, lam_k1, lam_q2, lam_k2,
              g_subln, w_conv, w_pool, s_pool, w_out, g_norm2, w_gate_up, w_down, g_final):
    b, seq_len, _ = x.shape
    n_blocks = seq_len // BLOCK_Q
    cos, sin = axial_rope_tables(seq_len, x.dtype)
    s_lat = jax.nn.silu(c)
    s_ctx = jax.nn.silu(c_ctx)
    x_lat, x_ctx = x, ctx
    for l in range(DEPTH):
        last = l == DEPTH - 1
        lam_init = 0.8 - 0.6 * math.exp(-0.3 * l)
        lam = (jnp.exp(jnp.sum(lam_q1[l] * lam_k1[l]).astype(jnp.float32))
               - jnp.exp(jnp.sum(lam_q2[l] * lam_k2[l]).astype(jnp.float32)) + lam_init)

        mod = s_lat @ w_ada[l] + b_ada[l]
        sh1, sc1, gt1, sh2, sc2, gt2 = jnp.split(mod[:, None, :], N_MOD, axis=-1)
        n_ctx_mod = 2 if last else N_MOD
        mod_c = s_ctx @ w_ada[l][:, :n_ctx_mod * D_MODEL] + b_ada[l][:n_ctx_mod * D_MODEL]
        mods_c = jnp.split(mod_c, n_ctx_mod)

        h = modulate(rms_norm(x_lat, g_norm1[l]), sh1, sc1)
        hc = modulate(rms_norm(x_ctx, g_norm1[l]), mods_c[0], mods_c[1])

        proj = h @ w_in[l]
        kv_c = hc @ w_in[l][:, Q_END:V_END]
        k_c = split_heads_qk(kv_c[..., :K_COLS])
        v_c = split_heads_v(kv_c[..., K_COLS:])

        q_l = apply_rope(split_heads_qk(proj[..., :Q_END]), cos, sin)
        k_l = apply_rope(split_heads_qk(proj[..., Q_END:K_END]), cos, sin)
        v_l = split_heads_v(proj[..., K_END:V_END])
        k_all = jnp.concatenate([k_c, k_l], axis=3)
        v_all = jnp.concatenate([v_c, v_l], axis=2)
        qb = jnp.moveaxis(q_l.reshape(b, ATTN_HEADS, 2, n_blocks, BLOCK_Q, ATTN_QK_DIM), 3, 0)
        o_l = lax.map(lambda qi: diff_attention_core(qi, k_all, v_all, lam), qb)
        o_l = jnp.moveaxis(o_l, 0, 2).reshape(b, ATTN_HEADS, seq_len, ATTN_V_DIM)

        mix = jnp.concatenate([
            diff_attn_post(o_l, g_subln[l], lam_init),
            conv_mixer(proj[..., V_END:CONV_END], w_conv[l]),
            pool_mixer(proj[..., CONV_END:], w_pool[l], s_pool[l]),
        ], axis=-1) @ w_out[l]
        x_lat = x_lat + gt1 * mix
        h2 = modulate(rms_norm(x_lat, g_norm2[l]), sh2, sc2)
        x_lat = x_lat + gt2 * swiglu(h2, w_gate_up[l], w_down[l])

        if not last:
            q_c = split_heads_qk(hc @ w_in[l][:, :Q_END])
            rest_c = hc @ w_in[l][:, V_END:]
            o_c = diff_attention_core(q_c, k_c, v_c, lam)
            mix_c = jnp.concatenate([
                diff_attn_post(o_c, g_subln[l], lam_init),
                conv_mixer(rest_c[..., :3 * CONV_WIDTH], w_conv[l]),
                pool_mixer(rest_c[..., 3 * CONV_WIDTH:], w_pool[l], s_pool[l]),
            ], axis=-1) @ w_out[l]
            x_ctx = x_ctx + mods_c[2] * mix_c
            h2c = modulate(rms_norm(x_ctx, g_norm2[l]), mods_c[3], mods_c[4])
            x_ctx = x_ctx + mods_c[5] * swiglu(h2c, w_gate_up[l], w_down[l])
    return rms_norm(x_lat, g_final)
```

```python
import functools
import math

import jax
import jax.numpy as jnp
from jax import lax
from jax.experimental import pallas as pl
from jax.experimental.pallas import tpu as pltpu

D_MODEL = 1024
DEPTH = 4
GRID_W = 64
N_MOD = 6
ATTN_HEADS = 4
ATTN_QK_DIM = 64
ATTN_V_DIM = 2 * ATTN_QK_DIM
HEAD_COLS = 2 * ATTN_QK_DIM
Q_COLS = ATTN_HEADS * HEAD_COLS
V_COLS = ATTN_HEADS * ATTN_V_DIM
ATTN_SCALE = ATTN_QK_DIM ** -0.5
ROPE_THETA = 10000.0
ROPE_HALF = ATTN_QK_DIM // 4
CONV_WIDTH = D_MODEL // 4
POOL_WINDOWS = (2, 4, 8, 16)
POOL_GROUPS = len(POOL_WINDOWS)
POOL_WIDTH = D_MODEL - V_COLS - CONV_WIDTH
POOL_GROUP_DIM = POOL_WIDTH // POOL_GROUPS
CP_COLS = 3 * CONV_WIDTH + POOL_WIDTH
IN_COLS = 3 * Q_COLS + CP_COLS
FFN_HIDDEN = -(-(8 * D_MODEL) // (3 * 256)) * 256
EPS = 1e-6

V7X_LANES = 128
V7X_MXU_COLS = 256
V7X_BF16_SUBLANES = 16
V7X_VMEM_LIMIT_BYTES = 56 * 1024 * 1024

MOD_ROWS = 40
HALO = V7X_BF16_SUBLANES
FFN_CHUNK = V7X_MXU_COLS
N_FFN_CHUNKS = FFN_HIDDEN // FFN_CHUNK
ADA_COL_BLOCK = 1536
Q_PRESCALE = ATTN_SCALE * math.log2(math.e)

F32 = jnp.float32
BF16 = jnp.bfloat16


def _dot(a, b):
    return jnp.dot(a, b, preferred_element_type=F32)


def _rms(x):
    return x * lax.rsqrt(jnp.mean(x * x, axis=-1, keepdims=True) + EPS)


def _ada_kernel(cc_ref, w_ref, b_ref, out_ref):
    cc = cc_ref[...]
    s = cc * jax.nn.sigmoid(cc)
    out_ref[0] = _dot(s.astype(BF16), w_ref[0].astype(BF16)) + b_ref[0]


def _ada_rows(cc, w_ada, b_ada):
    n_col = N_MOD * D_MODEL
    return pl.pallas_call(
        _ada_kernel,
        grid=(DEPTH, n_col // ADA_COL_BLOCK),
        in_specs=[
            pl.BlockSpec((MOD_ROWS, D_MODEL), lambda l, j: (0, 0)),
            pl.BlockSpec((1, D_MODEL, ADA_COL_BLOCK), lambda l, j: (l, 0, j)),
            pl.BlockSpec((1, 1, ADA_COL_BLOCK), lambda l, j: (l, 0, j)),
        ],
        out_specs=pl.BlockSpec((1, MOD_ROWS, ADA_COL_BLOCK), lambda l, j: (l, 0, j)),
        out_shape=jax.ShapeDtypeStruct((DEPTH, MOD_ROWS, n_col), F32),
        compiler_params=pltpu.CompilerParams(
            dimension_semantics=("arbitrary", "arbitrary"),
            vmem_limit_bytes=V7X_VMEM_LIMIT_BYTES),
        name="ada_rows",
    )(cc, w_ada, b_ada.reshape(DEPTH, 1, n_col))


def _rope_tables(length):
    pos = jnp.arange(length)
    row = (pos // GRID_W).astype(F32)
    col = (pos % GRID_W).astype(F32)
    inv = 1.0 / (ROPE_THETA ** (jnp.arange(ROPE_HALF, dtype=F32) / ROPE_HALF))
    ang_r = row[:, None] * inv[None, :]
    ang_c = col[:, None] * inv[None, :]
    ang = jnp.concatenate([ang_r, ang_r, ang_c, ang_c], axis=-1)
    cos = jnp.tile(jnp.cos(ang).astype(F32), (1, HEAD_COLS // ATTN_QK_DIM))
    sin = jnp.tile(jnp.sin(ang).astype(F32), (1, HEAD_COLS // ATTN_QK_DIM))
    first_half = (jnp.arange(HEAD_COLS) % (2 * ROPE_HALF)) < ROPE_HALF
    sin_from_upper = jnp.where(first_half, -sin, 0.0)
    sin_from_lower = jnp.where(first_half, 0.0, sin)
    return cos, sin_from_upper, sin_from_lower


def _inproj_kernel(*refs, rope, kv_only):
    x_ref, sh_ref, sc_ref, g_ref, w_ref = refs[:5]
    refs = refs[5:]
    if rope:
        cos_ref, sup_ref, slo_ref = refs[:3]
        refs = refs[3:]
    if kv_only:
        k_ref, vt_ref = refs
    else:
        q_ref, k_ref, vt_ref, cp_ref = refs

    h = (_rms(x_ref[0]) * g_ref[...]) * (1.0 + sc_ref[0]) + sh_ref[0]
    hb = h.astype(BF16)

    def rotate(t, scale):
        if not rope:
            return t if scale == 1.0 else t * scale
        out = []
        for j in range(Q_COLS // HEAD_COLS):
            tj = t[:, j * HEAD_COLS:(j + 1) * HEAD_COLS]
            rj = (tj * cos_ref[...]
                  + pltpu.roll(tj, HEAD_COLS - ROPE_HALF, 1) * sup_ref[...]
                  + pltpu.roll(tj, ROPE_HALF, 1) * slo_ref[...])
            out.append(rj if scale == 1.0 else rj * scale)
        return jnp.concatenate(out, axis=1)

    if not kv_only:
        q = _dot(hb, w_ref[:, 0:Q_COLS])
        q_ref[0] = rotate(q, Q_PRESCALE).astype(BF16)
    k = _dot(hb, w_ref[:, Q_COLS:2 * Q_COLS])
    k_ref[0] = rotate(k, 1.0).astype(BF16)
    v = _dot(hb, w_ref[:, 2 * Q_COLS:3 * Q_COLS])
    vt_ref[0] = v.T.astype(BF16)
    if not kv_only:
        cp_ref[0] = _dot(hb, w_ref[:, 3 * Q_COLS:IN_COLS]).astype(BF16)


def _inproj(x, shift, scale, mod_row, g, w_in, tables, *, tm, kv_only=False):
    b, length, _ = x.shape
    rope = tables is not None
    row = (lambda bi, i: (bi, 0, 0)) if mod_row is None else (lambda bi, i: (mod_row, 0, 0))
    in_specs = [
        pl.BlockSpec((1, tm, D_MODEL), lambda bi, i: (bi, i, 0)),
        pl.BlockSpec((1, 1, D_MODEL), row),
        pl.BlockSpec((1, 1, D_MODEL), row),
        pl.BlockSpec((1, D_MODEL), lambda bi, i: (0, 0)),
        pl.BlockSpec((D_MODEL, IN_COLS), lambda bi, i: (0, 0), pipeline_mode=pl.Buffered(1)),
    ]
    args = [x, shift, scale, g, w_in]
    if rope:
        in_specs += [pl.BlockSpec((tm, HEAD_COLS), lambda bi, i: (i, 0))] * 3
        args += list(tables)
    row_spec = lambda cols: pl.BlockSpec((1, tm, cols), lambda bi, i: (bi, i, 0))
    row_shape = lambda cols: jax.ShapeDtypeStruct((b, length, cols), BF16)
    vt_spec = pl.BlockSpec((1, V_COLS, tm), lambda bi, i: (bi, 0, i))
    vt_shape = jax.ShapeDtypeStruct((b, V_COLS, length), BF16)
    if kv_only:
        out_specs = [row_spec(Q_COLS), vt_spec]
        out_shape = [row_shape(Q_COLS), vt_shape]
    else:
        out_specs = [row_spec(Q_COLS), row_spec(Q_COLS), vt_spec, row_spec(CP_COLS)]
        out_shape = [row_shape(Q_COLS), row_shape(Q_COLS), vt_shape, row_shape(CP_COLS)]
    return pl.pallas_call(
        functools.partial(_inproj_kernel, rope=rope, kv_only=kv_only),
        grid=(b, length // tm),
        in_specs=in_specs,
        out_specs=out_specs,
        out_shape=out_shape,
        compiler_params=pltpu.CompilerParams(
            dimension_semantics=("arbitrary", "arbitrary"),
            vmem_limit_bytes=V7X_VMEM_LIMIT_BYTES),
        name="inproj_lat" if rope else "inproj_ctx",
    )(*args)


def _attn_kernel(*refs, key_lens, lam_init, tq):
    n_src = len(key_lens)
    lam_ref, g_ref, q_ref = refs[:3]
    k_refs = refs[3:3 + n_src]
    vt_refs = refs[3 + n_src:3 + 2 * n_src]
    o_ref = refs[3 + 2 * n_src]
    s_refs = refs[4 + 2 * n_src:6 + 2 * n_src]
    p_refs = refs[6 + 2 * n_src:8 + 2 * n_src]

    lq = lam_ref[...]
    lam = (jnp.exp(jnp.sum(lq[0:1] * lq[1:2], axis=1, keepdims=True))
           - jnp.exp(jnp.sum(lq[2:3] * lq[3:4], axis=1, keepdims=True)) + lam_init)
    lane = lax.broadcasted_iota(jnp.int32, (tq, HEAD_COLS), 1)
    n_q = q_ref.shape[1] // tq

    def q_tile(i, carry):
        r0 = pl.multiple_of(i * tq, tq)
        q = q_ref[0, pl.ds(r0, tq), :]
        zero = jnp.zeros_like(q)
        q_maps = (jnp.where(lane < ATTN_QK_DIM, q, zero), jnp.where(lane >= ATTN_QK_DIM, q, zero))
        outs = []
        for c in range(2):
            off = 0
            for k_ref, lk in zip(k_refs, key_lens):
                s_refs[c][off:off + lk, :] = lax.dot_general(
                    k_ref[0], q_maps[c], (((1,), (1,)), ((), ())), preferred_element_type=F32)
                off += lk
            s = s_refs[c][...]
            e = jnp.exp2(s - jnp.max(s, axis=0, keepdims=True))
            denom = jnp.sum(e, axis=0, keepdims=True)
            p_refs[c][...] = e.astype(BF16)
            acc = None
            off = 0
            for vt_ref, lk in zip(vt_refs, key_lens):
                part = _dot(vt_ref[0], p_refs[c][off:off + lk, :])
                acc = part if acc is None else acc + part
                off += lk
            outs.append(acc / denom)
        o_t = outs[0] - lam * outs[1]
        y = o_t * lax.rsqrt(jnp.mean(o_t * o_t, axis=0, keepdims=True) + EPS)
        o = (y.T * g_ref[...]) * (1.0 - lam_init)
        o_ref[0, pl.ds(r0, tq), :] = o.astype(BF16)
        return carry

    lax.fori_loop(0, n_q, q_tile, 0)


def _attention(lam_rows, g_subln, q, ks, vts, *, lam_init, tq):
    b, lq, _ = q.shape
    key_lens = tuple(k.shape[1] for k in ks)
    lk_total = sum(key_lens)
    head_rows = lambda length: pl.BlockSpec((1, length, HEAD_COLS), lambda bi, h: (bi, 0, h))
    in_specs = [
        pl.BlockSpec((4, ATTN_QK_DIM), lambda bi, h: (0, 0)),
        pl.BlockSpec((1, ATTN_V_DIM), lambda bi, h: (0, 0)),
        head_rows(lq),
    ]
    in_specs += [head_rows(lk) for lk in key_lens]
    in_specs += [pl.BlockSpec((1, ATTN_V_DIM, lk), lambda bi, h: (bi, h, 0)) for lk in key_lens]
    return pl.pallas_call(
        functools.partial(_attn_kernel, key_lens=key_lens, lam_init=lam_init, tq=tq),
        grid=(b, ATTN_HEADS),
        in_specs=in_specs,
        out_specs=head_rows(lq),
        out_shape=jax.ShapeDtypeStruct((b, lq, V_COLS), BF16),
        scratch_shapes=[pltpu.VMEM((lk_total, tq), F32), pltpu.VMEM((lk_total, tq), F32),
                        pltpu.VMEM((lk_total, tq), BF16), pltpu.VMEM((lk_total, tq), BF16)],
        compiler_params=pltpu.CompilerParams(
            dimension_semantics=("arbitrary", "arbitrary"),
            vmem_limit_bytes=V7X_VMEM_LIMIT_BYTES),
        name="diff_attn_lat" if len(ks) == 2 else "diff_attn_ctx",
    )(lam_rows, g_subln, q, *ks, *vts)


def _mixffn_kernel(x_ref, o_ref, cp_prev_ref, cp_ref, cp_next_ref, gt1_ref, sh2_ref, sc2_ref, gt2_ref,
                   g2_ref, wconv_ref, wpool_ref, spool_ref, wout_ref, wgu_ref, wd_ref, gfin_ref,
                   out_ref, acc_ref, *, length, tm, final):
    i = pl.program_id(1)
    n_tiles = length // tm
    rows = tm + 2 * HALO

    prev = jnp.where(i > 0, cp_prev_ref[0].astype(F32), 0.0)
    nxt = jnp.where(i < n_tiles - 1, cp_next_ref[0].astype(F32), 0.0)
    cur = cp_ref[0].astype(F32)
    ext = jnp.concatenate([prev, cur, nxt], axis=0)
    core = lambda a: a[HALO:HALO + tm]

    def shifted(a, d):
        return pltpu.roll(a, d % rows, 0)

    u = ext[:, CONV_WIDTH:2 * CONV_WIDTH] * ext[:, 2 * CONV_WIDTH:3 * CONV_WIDTH]
    wc = wconv_ref[...]
    y = shifted(u, 1) * wc[0:1] + u * wc[1:2] + shifted(u, -1) * wc[2:3]
    conv = cur[:, 0:CONV_WIDTH] * core(y)

    t = i * tm + lax.broadcasted_iota(jnp.int32, (tm, V7X_LANES), 0)
    lane = lax.broadcasted_iota(jnp.int32, (tm, V7X_LANES), 1)
    pooled = []
    for tile in range(POOL_WIDTH // V7X_LANES):
        p = ext[:, 3 * CONV_WIDTH + tile * V7X_LANES:3 * CONV_WIDTH + (tile + 1) * V7X_LANES]
        sums = {2: p + shifted(p, 1)}
        for w in (4, 8, 16):
            sums[w] = shifted(sums[w // 2], w // 4) + shifted(sums[w // 2], -(w // 4))
        w_lo, w_hi = POOL_WINDOWS[2 * tile], POOL_WINDOWS[2 * tile + 1]
        lower_group = lane < POOL_GROUP_DIM
        half = jnp.where(lower_group, w_lo // 2, w_hi // 2)
        cnt = jnp.minimum(t + half - 1, length - 1) - jnp.maximum(t - half, 0) + 1
        win_sum = jnp.where(lower_group, core(sums[w_lo]), core(sums[w_hi]))
        pooled.append(win_sum / cnt.astype(F32) - core(p))
    pooled = jnp.concatenate(pooled, axis=1).astype(BF16)
    pool_y = _dot(pooled, wpool_ref[...]) * spool_ref[...]

    wo = wout_ref
    mix = (_dot(o_ref[0], wo[0:V_COLS, :])
           + _dot(conv.astype(BF16), wo[V_COLS:V_COLS + CONV_WIDTH, :])
           + _dot(pool_y.astype(BF16), wo[V_COLS + CONV_WIDTH:D_MODEL, :]))
    x1 = x_ref[0] + gt1_ref[0] * mix
    out_ref[0] = x1
    h2 = ((_rms(x1) * g2_ref[...]) * (1.0 + sc2_ref[0]) + sh2_ref[0]).astype(BF16)

    for c in range(N_FFN_CHUNKS):
        gu = _dot(h2, wgu_ref[:, 2 * c * FFN_CHUNK:2 * (c + 1) * FFN_CHUNK])
        gate, up = gu[:, :FFN_CHUNK], gu[:, FFN_CHUNK:]
        act = (gate * jax.nn.sigmoid(gate) * up).astype(BF16)
        part = _dot(act, wd_ref[c * FFN_CHUNK:(c + 1) * FFN_CHUNK, :])
        if c == 0:
            acc_ref[...] = part
        else:
            acc_ref[...] += part

    x2 = out_ref[0] + gt2_ref[0] * acc_ref[...]
    if final:
        x2 = _rms(x2) * gfin_ref[...]
    out_ref[0] = x2


def _mixffn(x, o, cp, mods, mod_row, g2, wconv, wpool_bd, spool, wout, wgu, wd, gfin, *, tm, final):
    b, length, _ = x.shape
    n_tiles = length // tm
    halo_per_tile = tm // HALO
    n_halo_blocks = length // HALO
    row = (lambda bi, i: (bi, 0, 0)) if mod_row is None else (lambda bi, i: (mod_row, 0, 0))
    const = lambda shape: pl.BlockSpec(shape, lambda bi, i: (0,) * len(shape))
    weight = lambda shape: pl.BlockSpec(shape, lambda bi, i: (0,) * len(shape), pipeline_mode=pl.Buffered(1))
    in_specs = [
        pl.BlockSpec((1, tm, D_MODEL), lambda bi, i: (bi, i, 0)),
        pl.BlockSpec((1, tm, V_COLS), lambda bi, i: (bi, i, 0)),
        pl.BlockSpec((1, HALO, CP_COLS), lambda bi, i: (bi, jnp.maximum(i * halo_per_tile - 1, 0), 0)),
        pl.BlockSpec((1, tm, CP_COLS), lambda bi, i: (bi, i, 0)),
        pl.BlockSpec((1, HALO, CP_COLS),
                     lambda bi, i: (bi, jnp.minimum((i + 1) * halo_per_tile, n_halo_blocks - 1), 0)),
    ] + [pl.BlockSpec((1, 1, D_MODEL), row)] * 4 + [
        const((1, D_MODEL)),
        const((3, CONV_WIDTH)),
        weight((POOL_WIDTH, POOL_WIDTH)),
        const((1, POOL_WIDTH)),
        weight((D_MODEL, D_MODEL)),
        weight((D_MODEL, 2 * FFN_HIDDEN)),
        weight((FFN_HIDDEN, D_MODEL)),
        const((1, D_MODEL)),
    ]
    return pl.pallas_call(
        functools.partial(_mixffn_kernel, length=length, tm=tm, final=final),
        grid=(b, n_tiles),
        in_specs=in_specs,
        out_specs=pl.BlockSpec((1, tm, D_MODEL), lambda bi, i: (bi, i, 0)),
        out_shape=jax.ShapeDtypeStruct((b, length, D_MODEL), F32),
        scratch_shapes=[pltpu.VMEM((tm, D_MODEL), F32)],
        compiler_params=pltpu.CompilerParams(
            dimension_semantics=("arbitrary", "arbitrary"),
            vmem_limit_bytes=V7X_VMEM_LIMIT_BYTES),
        name="mix_ffn_lat" if mod_row is None else "mix_ffn_ctx",
    )(x, o, cp, cp, cp, *mods, g2, wconv, wpool_bd, spool, wout, wgu, wd, gfin)


@jax.jit
def _forward(x, c, ctx, c_ctx, w_ada, b_ada, g_norm1, w_in, lam_q1, lam_k1, lam_q2, lam_k2,
             g_subln, w_conv, w_pool, s_pool, w_out, g_norm2, w_gate_up, w_down, g_final):
    batch, seq_len, _ = x.shape
    ctx_len = ctx.shape[1]
    ctx_row = batch

    cc = jnp.concatenate([c, c_ctx[None, :], jnp.zeros((MOD_ROWS - batch - 1, D_MODEL), F32)], axis=0)
    mod = _ada_rows(cc, w_ada, b_ada).reshape(DEPTH, MOD_ROWS, N_MOD, 1, D_MODEL)

    w_in_b = w_in.astype(BF16)
    w_out_b = w_out.astype(BF16)
    w_gu_b = (w_gate_up.reshape(DEPTH, D_MODEL, 2, N_FFN_CHUNKS, FFN_CHUNK)
              .transpose(0, 1, 3, 2, 4).reshape(DEPTH, D_MODEL, 2 * FFN_HIDDEN).astype(BF16))
    w_down_b = w_down.astype(BF16)
    eye = jnp.eye(POOL_GROUPS, dtype=F32)
    w_pool_bd = (w_pool[:, :, :, None, :] * eye[None, :, None, :, None]).reshape(
        DEPTH, POOL_WIDTH, POOL_WIDTH).astype(BF16)
    tables = _rope_tables(seq_len)

    x_lat, x_ctx = x, ctx
    tm_lat, tm_ctx = min(512, seq_len), ctx_len
    for l in range(DEPTH):
        last = l == DEPTH - 1
        lam_init = 0.8 - 0.6 * math.exp(-0.3 * l)
        lam_rows = jnp.stack([lam_q1[l], lam_k1[l], lam_q2[l], lam_k2[l]])
        m = [mod[l, :, j] for j in range(N_MOD)]
        g1 = g_norm1[l][None, :]
        g2 = g_norm2[l][None, :]
        gs = g_subln[l][None, :]

        ctx_proj = _inproj(x_ctx, m[0], m[1], ctx_row, g1, w_in_b[l], None, tm=tm_ctx, kv_only=last)
        q_l, k_l, vt_l, cp_l = _inproj(x_lat, m[0], m[1], None, g1, w_in_b[l], tables, tm=tm_lat)
        k_c, vt_c = ctx_proj[-3:-1] if not last else ctx_proj
        o_l = _attention(lam_rows, gs, q_l, [k_c, k_l], [vt_c, vt_l], lam_init=lam_init, tq=256)
        mix_args = (g2, w_conv[l], w_pool_bd[l], s_pool[l][None, :], w_out_b[l], w_gu_b[l], w_down_b[l],
                    g_final[None, :])
        x_lat = _mixffn(x_lat, o_l, cp_l, m[2:6], None, *mix_args, tm=tm_lat, final=last)
        if not last:
            q_c, _, _, cp_c = ctx_proj
            o_c = _attention(lam_rows, gs, q_c, [k_c], [vt_c], lam_init=lam_init, tq=ctx_len)
            x_ctx = _mixffn(x_ctx, o_c, cp_c, m[2:6], ctx_row, *mix_args, tm=tm_ctx, final=False)
    return x_lat


def kernel(x, c, ctx, c_ctx, w_ada, b_ada, g_norm1, w_in, lam_q1, lam_k1, lam_q2, lam_k2, g_subln, w_conv,
           w_pool, s_pool, w_out, g_norm2, w_gate_up, w_down, g_final):
    return _forward(x, c, ctx, c_ctx, w_ada, b_ada, g_norm1, w_in, lam_q1, lam_k1, lam_q2, lam_k2, g_subln,
                    w_conv, w_pool, s_pool, w_out, g_norm2, w_gate_up, w_down, g_final)
```

```python
import functools
import math

import jax
import jax.numpy as jnp
from jax import lax
from jax.experimental import pallas as pl
from jax.experimental.pallas import tpu as pltpu

D_MODEL = 1024
DEPTH = 4
GRID_W = 64
N_MOD = 6
ATTN_HEADS = 4
ATTN_QK_DIM = 64
ATTN_V_DIM = 2 * ATTN_QK_DIM
HEAD_COLS = 2 * ATTN_QK_DIM
Q_COLS = ATTN_HEADS * HEAD_COLS
V_COLS = ATTN_HEADS * ATTN_V_DIM
ATTN_SCALE = ATTN_QK_DIM ** -0.5
ROPE_THETA = 10000.0
ROPE_HALF = ATTN_QK_DIM // 4
CONV_WIDTH = D_MODEL // 4
POOL_WINDOWS = (2, 4, 8, 16)
POOL_GROUPS = len(POOL_WINDOWS)
POOL_WIDTH = D_MODEL - V_COLS - CONV_WIDTH
POOL_GROUP_DIM = POOL_WIDTH // POOL_GROUPS
CP_COLS = 3 * CONV_WIDTH + POOL_WIDTH
IN_COLS = 3 * Q_COLS + CP_COLS
FFN_HIDDEN = -(-(8 * D_MODEL) // (3 * 256)) * 256
EPS = 1e-6

V7X_LANES = 128
V7X_MXU_COLS = 256
V7X_BF16_SUBLANES = 16
V7X_VMEM_LIMIT_BYTES = 56 * 1024 * 1024

MOD_ROWS = 40
HALO = V7X_BF16_SUBLANES
FFN_CHUNK = V7X_MXU_COLS
N_FFN_CHUNKS = FFN_HIDDEN // FFN_CHUNK
ADA_COL_BLOCK = 1536
Q_PRESCALE = ATTN_SCALE * math.log2(math.e)

F32 = jnp.float32
BF16 = jnp.bfloat16


def _dot(a, b):
    return jnp.dot(a, b, preferred_element_type=F32)


def _rms(x):
    return x * lax.rsqrt(jnp.mean(x * x, axis=-1, keepdims=True) + EPS)


def _ada_kernel(cc_ref, w_ref, b_ref, out_ref):
    cc = cc_ref[...]
    s = cc * jax.nn.sigmoid(cc)
    out_ref[0] = _dot(s.astype(BF16), w_ref[0].astype(BF16)) + b_ref[0]


def _ada_rows(cc, w_ada, b_ada):
    n_col = N_MOD * D_MODEL
    return pl.pallas_call(
        _ada_kernel,
        grid=(DEPTH, n_col // ADA_COL_BLOCK),
        in_specs=[
            pl.BlockSpec((MOD_ROWS, D_MODEL), lambda l, j: (0, 0)),
            pl.BlockSpec((1, D_MODEL, ADA_COL_BLOCK), lambda l, j: (l, 0, j)),
            pl.BlockSpec((1, 1, ADA_COL_BLOCK), lambda l, j: (l, 0, j)),
        ],
        out_specs=pl.BlockSpec((1, MOD_ROWS, ADA_COL_BLOCK), lambda l, j: (l, 0, j)),
        out_shape=jax.ShapeDtypeStruct((DEPTH, MOD_ROWS, n_col), F32),
        compiler_params=pltpu.CompilerParams(
            dimension_semantics=("arbitrary", "arbitrary"),
            vmem_limit_bytes=V7X_VMEM_LIMIT_BYTES),
        name="ada_rows",
    )(cc, w_ada, b_ada.reshape(DEPTH, 1, n_col))


def _rope_tables(length):
    pos = jnp.arange(length)
    row = (pos // GRID_W).astype(F32)
    col = (pos % GRID_W).astype(F32)
    inv = 1.0 / (ROPE_THETA ** (jnp.arange(ROPE_HALF, dtype=F32) / ROPE_HALF))
    ang_r = row[:, None] * inv[None, :]
    ang_c = col[:, None] * inv[None, :]
    ang = jnp.concatenate([ang_r, ang_r, ang_c, ang_c], axis=-1)
    cos = jnp.tile(jnp.cos(ang).astype(F32), (1, HEAD_COLS // ATTN_QK_DIM))
    sin = jnp.tile(jnp.sin(ang).astype(F32), (1, HEAD_COLS // ATTN_QK_DIM))
    first_half = (jnp.arange(HEAD_COLS) % (2 * ROPE_HALF)) < ROPE_HALF
    sin_from_upper = jnp.where(first_half, -sin, 0.0)
    sin_from_lower = jnp.where(first_half, 0.0, sin)
    return cos, sin_from_upper, sin_from_lower


def _inproj_kernel(*refs, rope, kv_only):
    x_ref, sh_ref, sc_ref, g_ref, w_ref = refs[:5]
    refs = refs[5:]
    if rope:
        cos_ref, sup_ref, slo_ref = refs[:3]
        refs = refs[3:]
    if kv_only:
        k_ref, vt_ref = refs
    else:
        q_ref, k_ref, vt_ref, cp_ref = refs

    h = (_rms(x_ref[0]) * g_ref[...]) * (1.0 + sc_ref[0]) + sh_ref[0]
    hb = h.astype(BF16)

    def rotate(t, scale):
        if not rope:
            return t if scale == 1.0 else t * scale
        out = []
        for j in range(Q_COLS // HEAD_COLS):
            tj = t[:, j * HEAD_COLS:(j + 1) * HEAD_COLS]
            rj = (tj * cos_ref[...]
                  + pltpu.roll(tj, HEAD_COLS - ROPE_HALF, 1) * sup_ref[...]
                  + pltpu.roll(tj, ROPE_HALF, 1) * slo_ref[...])
            out.append(rj if scale == 1.0 else rj * scale)
        return jnp.concatenate(out, axis=1)

    if not kv_only:
        q = _dot(hb, w_ref[:, 0:Q_COLS])
        q_ref[0] = rotate(q, Q_PRESCALE).astype(BF16)
    k = _dot(hb, w_ref[:, Q_COLS:2 * Q_COLS])
    k_ref[0] = rotate(k, 1.0).astype(BF16)
    v = _dot(hb, w_ref[:, 2 * Q_COLS:3 * Q_COLS])
    vt_ref[0] = v.T.astype(BF16)
    if not kv_only:
        cp_ref[0] = _dot(hb, w_ref[:, 3 * Q_COLS:IN_COLS]).astype(BF16)


def _inproj(x, shift, scale, mod_row, g, w_in, tables, *, tm, kv_only=False):
    b, length, _ = x.shape
    rope = tables is not None
    row = (lambda bi, i: (bi, 0, 0)) if mod_row is None else (lambda bi, i: (mod_row, 0, 0))
    in_specs = [
        pl.BlockSpec((1, tm, D_MODEL), lambda bi, i: (bi, i, 0)),
        pl.BlockSpec((1, 1, D_MODEL), row),
        pl.BlockSpec((1, 1, D_MODEL), row),
        pl.BlockSpec((1, D_MODEL), lambda bi, i: (0, 0)),
        pl.BlockSpec((D_MODEL, IN_COLS), lambda bi, i: (0, 0), pipeline_mode=pl.Buffered(1)),
    ]
    args = [x, shift, scale, g, w_in]
    if rope:
        in_specs += [pl.BlockSpec((tm, HEAD_COLS), lambda bi, i: (i, 0))] * 3
        args += list(tables)
    row_spec = lambda cols: pl.BlockSpec((1, tm, cols), lambda bi, i: (bi, i, 0))
    row_shape = lambda cols: jax.ShapeDtypeStruct((b, length, cols), BF16)
    vt_spec = pl.BlockSpec((1, V_COLS, tm), lambda bi, i: (bi, 0, i))
    vt_shape = jax.ShapeDtypeStruct((b, V_COLS, length), BF16)
    if kv_only:
        out_specs = [row_spec(Q_COLS), vt_spec]
        out_shape = [row_shape(Q_COLS), vt_shape]
    else:
        out_specs = [row_spec(Q_COLS), row_spec(Q_COLS), vt_spec, row_spec(CP_COLS)]
        out_shape = [row_shape(Q_COLS), row_shape(Q_COLS), vt_shape, row_shape(CP_COLS)]
    return pl.pallas_call(
        functools.partial(_inproj_kernel, rope=rope, kv_only=kv_only),
        grid=(b, length // tm),
        in_specs=in_specs,
        out_specs=out_specs,
        out_shape=out_shape,
        compiler_params=pltpu.CompilerParams(
            dimension_semantics=("arbitrary", "arbitrary"),
            vmem_limit_bytes=V7X_VMEM_LIMIT_BYTES),
        name="inproj_lat" if rope else "inproj_ctx",
    )(*args)


def _attn_kernel(*refs, key_lens, lam_init, tq):
    n_src = len(key_lens)
    lam_ref, g_ref, q_ref = refs[:3]
    k_refs = refs[3:3 + n_src]
    vt_refs = refs[3 + n_src:3 + 2 * n_src]
    o_ref = refs[3 + 2 * n_src]
    k_all, vt_all, s_ref, p_ref = refs[4 + 2 * n_src:]

    lq = lam_ref[...]
    lam = (jnp.exp(jnp.sum(lq[0:1] * lq[1:2], axis=1, keepdims=True))
           - jnp.exp(jnp.sum(lq[2:3] * lq[3:4], axis=1, keepdims=True)) + lam_init)
    lane = lax.broadcasted_iota(jnp.int32, (tq, HEAD_COLS), 1)
    n_q = q_ref.shape[1] // tq

    off = 0
    for k_ref, vt_ref, lk in zip(k_refs, vt_refs, key_lens):
        k_all[off:off + lk, :] = k_ref[0]
        vt_all[:, off:off + lk] = vt_ref[0]
        off += lk

    def scores(i, slot):
        q = q_ref[0, pl.ds(pl.multiple_of(i * tq, tq), tq), :]
        zero = jnp.zeros_like(q)
        maxima = []
        for c, keep in enumerate((lane < ATTN_QK_DIM, lane >= ATTN_QK_DIM)):
            s = lax.dot_general(k_all[...], jnp.where(keep, q, zero), (((1,), (1,)), ((), ())),
                                preferred_element_type=F32)
            s_ref[slot, c] = s
            maxima.append(jnp.max(s, axis=0, keepdims=True))
        return tuple(maxima)

    def finish(i, slot, maxima):
        outs = []
        for c in range(2):
            e = jnp.exp2(s_ref[slot, c] - maxima[c])
            denom = jnp.sum(e, axis=0, keepdims=True)
            p_ref[c] = e.astype(BF16)
            outs.append(_dot(vt_all[...], p_ref[c]) / denom)
        o_t = outs[0] - lam * outs[1]
        y = o_t * lax.rsqrt(jnp.mean(o_t * o_t, axis=0, keepdims=True) + EPS)
        o = (y.T * g_ref[...]) * (1.0 - lam_init)
        o_ref[0, pl.ds(pl.multiple_of(i * tq, tq), tq), :] = o.astype(BF16)

    maxima = scores(0, 0)
    if n_q > 1:
        assert n_q % 2 == 0

        def pair(j, maxima0):
            maxima1 = scores(2 * j + 1, 1)
            finish(2 * j, 0, maxima0)
            maxima2 = scores(2 * j + 2, 0)
            finish(2 * j + 1, 1, maxima1)
            return maxima2

        maxima = lax.fori_loop(0, n_q // 2 - 1, pair, maxima)
        maxima_last = scores(n_q - 1, 1)
        finish(n_q - 2, 0, maxima)
        finish(n_q - 1, 1, maxima_last)
    else:
        finish(0, 0, maxima)


def _attention(lam_rows, g_subln, q, ks, vts, *, lam_init, tq):
    b, lq, _ = q.shape
    key_lens = tuple(k.shape[1] for k in ks)
    lk_total = sum(key_lens)
    head_rows = lambda length: pl.BlockSpec((1, length, HEAD_COLS), lambda bi, h: (bi, 0, h))
    in_specs = [
        pl.BlockSpec((4, ATTN_QK_DIM), lambda bi, h: (0, 0)),
        pl.BlockSpec((1, ATTN_V_DIM), lambda bi, h: (0, 0)),
        head_rows(lq),
    ]
    in_specs += [head_rows(lk) for lk in key_lens]
    in_specs += [pl.BlockSpec((1, ATTN_V_DIM, lk), lambda bi, h: (bi, h, 0)) for lk in key_lens]
    return pl.pallas_call(
        functools.partial(_attn_kernel, key_lens=key_lens, lam_init=lam_init, tq=tq),
        grid=(b, ATTN_HEADS),
        in_specs=in_specs,
        out_specs=head_rows(lq),
        out_shape=jax.ShapeDtypeStruct((b, lq, V_COLS), BF16),
        scratch_shapes=[pltpu.VMEM((lk_total, HEAD_COLS), BF16), pltpu.VMEM((ATTN_V_DIM, lk_total), BF16),
                        pltpu.VMEM((2, 2, lk_total, tq), F32), pltpu.VMEM((2, lk_total, tq), BF16)],
        compiler_params=pltpu.CompilerParams(
            dimension_semantics=("arbitrary", "arbitrary"),
            vmem_limit_bytes=V7X_VMEM_LIMIT_BYTES),
        name="diff_attn_lat" if len(ks) == 2 else "diff_attn_ctx",
    )(lam_rows, g_subln, q, *ks, *vts)


def _mixffn_kernel(x_ref, o_ref, cp_prev_ref, cp_ref, cp_next_ref, gt1_ref, sh2_ref, sc2_ref, gt2_ref,
                   g2_ref, wconv_ref, wpool_ref, spool_ref, wout_ref, wgu_ref, wd_ref, gfin_ref,
                   out_ref, acc_ref, *, length, tm, final):
    i = pl.program_id(1)
    n_tiles = length // tm
    rows = tm + 2 * HALO

    prev = jnp.where(i > 0, cp_prev_ref[0].astype(F32), 0.0)
    nxt = jnp.where(i < n_tiles - 1, cp_next_ref[0].astype(F32), 0.0)
    cur = cp_ref[0].astype(F32)
    ext = jnp.concatenate([prev, cur, nxt], axis=0)
    core = lambda a: a[HALO:HALO + tm]

    def shifted(a, d):
        return pltpu.roll(a, d % rows, 0)

    u = ext[:, CONV_WIDTH:2 * CONV_WIDTH] * ext[:, 2 * CONV_WIDTH:3 * CONV_WIDTH]
    wc = wconv_ref[...]
    y = shifted(u, 1) * wc[0:1] + u * wc[1:2] + shifted(u, -1) * wc[2:3]
    conv = cur[:, 0:CONV_WIDTH] * core(y)

    t = i * tm + lax.broadcasted_iota(jnp.int32, (tm, V7X_LANES), 0)
    lane = lax.broadcasted_iota(jnp.int32, (tm, V7X_LANES), 1)
    pooled = []
    for tile in range(POOL_WIDTH // V7X_LANES):
        p = ext[:, 3 * CONV_WIDTH + tile * V7X_LANES:3 * CONV_WIDTH + (tile + 1) * V7X_LANES]
        sums = {2: p + shifted(p, 1)}
        for w in (4, 8, 16):
            sums[w] = shifted(sums[w // 2], w // 4) + shifted(sums[w // 2], -(w // 4))
        w_lo, w_hi = POOL_WINDOWS[2 * tile], POOL_WINDOWS[2 * tile + 1]
        lower_group = lane < POOL_GROUP_DIM
        half = jnp.where(lower_group, w_lo // 2, w_hi // 2)
        cnt = jnp.minimum(t + half - 1, length - 1) - jnp.maximum(t - half, 0) + 1
        win_sum = jnp.where(lower_group, core(sums[w_lo]), core(sums[w_hi]))
        pooled.append(win_sum / cnt.astype(F32) - core(p))
    pooled = jnp.concatenate(pooled, axis=1).astype(BF16)
    pool_y = _dot(pooled, wpool_ref[...]) * spool_ref[...]

    wo = wout_ref
    mix = (_dot(o_ref[0], wo[0:V_COLS, :])
           + _dot(conv.astype(BF16), wo[V_COLS:V_COLS + CONV_WIDTH, :])
           + _dot(pool_y.astype(BF16), wo[V_COLS + CONV_WIDTH:D_MODEL, :]))
    x1 = x_ref[0] + gt1_ref[0] * mix
    out_ref[0] = x1
    h2 = ((_rms(x1) * g2_ref[...]) * (1.0 + sc2_ref[0]) + sh2_ref[0]).astype(BF16)

    for c in range(N_FFN_CHUNKS):
        gu = _dot(h2, wgu_ref[:, 2 * c * FFN_CHUNK:2 * (c + 1) * FFN_CHUNK])
        gate, up = gu[:, :FFN_CHUNK], gu[:, FFN_CHUNK:]
        act = (gate * jax.nn.sigmoid(gate) * up).astype(BF16)
        part = _dot(act, wd_ref[c * FFN_CHUNK:(c + 1) * FFN_CHUNK, :])
        if c == 0:
            acc_ref[...] = part
        else:
            acc_ref[...] += part

    x2 = out_ref[0] + gt2_ref[0] * acc_ref[...]
    if final:
        x2 = _rms(x2) * gfin_ref[...]
    out_ref[0] = x2


def _mixffn(x, o, cp, mods, mod_row, g2, wconv, wpool_bd, spool, wout, wgu, wd, gfin, *, tm, final):
    b, length, _ = x.shape
    n_tiles = length // tm
    halo_per_tile = tm // HALO
    n_halo_blocks = length // HALO
    row = (lambda bi, i: (bi, 0, 0)) if mod_row is None else (lambda bi, i: (mod_row, 0, 0))
    const = lambda shape: pl.BlockSpec(shape, lambda bi, i: (0,) * len(shape))
    weight = lambda shape: pl.BlockSpec(shape, lambda bi, i: (0,) * len(shape), pipeline_mode=pl.Buffered(1))
    in_specs = [
        pl.BlockSpec((1, tm, D_MODEL), lambda bi, i: (bi, i, 0)),
        pl.BlockSpec((1, tm, V_COLS), lambda bi, i: (bi, i, 0)),
        pl.BlockSpec((1, HALO, CP_COLS), lambda bi, i: (bi, jnp.maximum(i * halo_per_tile - 1, 0), 0)),
        pl.BlockSpec((1, tm, CP_COLS), lambda bi, i: (bi, i, 0)),
        pl.BlockSpec((1, HALO, CP_COLS),
                     lambda bi, i: (bi, jnp.minimum((i + 1) * halo_per_tile, n_halo_blocks - 1), 0)),
    ] + [pl.BlockSpec((1, 1, D_MODEL), row)] * 4 + [
        const((1, D_MODEL)),
        const((3, CONV_WIDTH)),
        weight((POOL_WIDTH, POOL_WIDTH)),
        const((1, POOL_WIDTH)),
        weight((D_MODEL, D_MODEL)),
        weight((D_MODEL, 2 * FFN_HIDDEN)),
        weight((FFN_HIDDEN, D_MODEL)),
        const((1, D_MODEL)),
    ]
    return pl.pallas_call(
        functools.partial(_mixffn_kernel, length=length, tm=tm, final=final),
        grid=(b, n_tiles),
        in_specs=in_specs,
        out_specs=pl.BlockSpec((1, tm, D_MODEL), lambda bi, i: (bi, i, 0)),
        out_shape=jax.ShapeDtypeStruct((b, length, D_MODEL), F32),
        scratch_shapes=[pltpu.VMEM((tm, D_MODEL), F32)],
        compiler_params=pltpu.CompilerParams(
            dimension_semantics=("arbitrary", "arbitrary"),
            vmem_limit_bytes=V7X_VMEM_LIMIT_BYTES),
        name="mix_ffn_lat" if mod_row is None else "mix_ffn_ctx",
    )(x, o, cp, cp, cp, *mods, g2, wconv, wpool_bd, spool, wout, wgu, wd, gfin)


@jax.jit
def _forward(x, c, ctx, c_ctx, w_ada, b_ada, g_norm1, w_in, lam_q1, lam_k1, lam_q2, lam_k2,
             g_subln, w_conv, w_pool, s_pool, w_out, g_norm2, w_gate_up, w_down, g_final):
    batch, seq_len, _ = x.shape
    ctx_len = ctx.shape[1]
    ctx_row = batch

    cc = jnp.concatenate([c, c_ctx[None, :], jnp.zeros((MOD_ROWS - batch - 1, D_MODEL), F32)], axis=0)
    mod = _ada_rows(cc, w_ada, b_ada).reshape(DEPTH, MOD_ROWS, N_MOD, 1, D_MODEL)

    w_in_b = w_in.astype(BF16)
    w_out_b = w_out.astype(BF16)
    w_gu_b = (w_gate_up.reshape(DEPTH, D_MODEL, 2, N_FFN_CHUNKS, FFN_CHUNK)
              .transpose(0, 1, 3, 2, 4).reshape(DEPTH, D_MODEL, 2 * FFN_HIDDEN).astype(BF16))
    w_down_b = w_down.astype(BF16)
    eye = jnp.eye(POOL_GROUPS, dtype=F32)
    w_pool_bd = (w_pool[:, :, :, None, :] * eye[None, :, None, :, None]).reshape(
        DEPTH, POOL_WIDTH, POOL_WIDTH).astype(BF16)
    tables = _rope_tables(seq_len)

    x_lat, x_ctx = x, ctx
    tm_lat, tm_ctx = min(512, seq_len), ctx_len
    for l in range(DEPTH):
        last = l == DEPTH - 1
        lam_init = 0.8 - 0.6 * math.exp(-0.3 * l)
        lam_rows = jnp.stack([lam_q1[l], lam_k1[l], lam_q2[l], lam_k2[l]])
        m = [mod[l, :, j] for j in range(N_MOD)]
        g1 = g_norm1[l][None, :]
        g2 = g_norm2[l][None, :]
        gs = g_subln[l][None, :]

        ctx_proj = _inproj(x_ctx, m[0], m[1], ctx_row, g1, w_in_b[l], None, tm=tm_ctx, kv_only=last)
        q_l, k_l, vt_l, cp_l = _inproj(x_lat, m[0], m[1], None, g1, w_in_b[l], tables, tm=tm_lat)
        k_c, vt_c = ctx_proj[-3:-1] if not last else ctx_proj
        o_l = _attention(lam_rows, gs, q_l, [k_c, k_l], [vt_c, vt_l], lam_init=lam_init, tq=256)
        mix_args = (g2, w_conv[l], w_pool_bd[l], s_pool[l][None, :], w_out_b[l], w_gu_b[l], w_down_b[l],
                    g_final[None, :])
        x_lat = _mixffn(x_lat, o_l, cp_l, m[2:6], None, *mix_args, tm=tm_lat, final=last)
        if not last:
            q_c, _, _, cp_c = ctx_proj
            o_c = _attention(lam_rows, gs, q_c, [k_c], [vt_c], lam_init=lam_init, tq=ctx_len)
            x_ctx = _mixffn(x_ctx, o_c, cp_c, m[2:6], ctx_row, *mix_args, tm=tm_ctx, final=False)
    return x_lat


def kernel(x, c, ctx, c_ctx, w_ada, b_ada, g_norm1, w_in, lam_q1, lam_k1, lam_q2, lam_k2, g_subln, w_conv,
           w_pool, s_pool, w_out, g_norm2, w_gate_up, w_down, g_final):
    return _forward(x, c, ctx, c_ctx, w_ada, b_ada, g_norm1, w_in, lam_q1, lam_k1, lam_q2, lam_k2, g_subln,
                    w_conv, w_pool, s_pool, w_out, g_norm2, w_gate_up, w_down, g_final)
```

```python
import functools
import math
from typing import NamedTuple

import jax
import jax.numpy as jnp
from jax import lax
from jax.experimental import pallas as pl
from jax.experimental.pallas import tpu as pltpu

D_MODEL = 1024
DEPTH = 4
GRID_W = 64
N_MOD = 6
ATTN_HEADS = 4
ATTN_QK_DIM = 64
ATTN_V_DIM = 2 * ATTN_QK_DIM
HEAD_COLS = 2 * ATTN_QK_DIM
Q_COLS = ATTN_HEADS * HEAD_COLS
V_COLS = ATTN_HEADS * ATTN_V_DIM
ATTN_SCALE = ATTN_QK_DIM ** -0.5
ROPE_THETA = 10000.0
ROPE_HALF = ATTN_QK_DIM // 4
CONV_WIDTH = D_MODEL // 4
POOL_WINDOWS = (2, 4, 8, 16)
POOL_GROUPS = len(POOL_WINDOWS)
POOL_WIDTH = D_MODEL - V_COLS - CONV_WIDTH
POOL_GROUP_DIM = POOL_WIDTH // POOL_GROUPS
CP_COLS = 3 * CONV_WIDTH + POOL_WIDTH
IN_COLS = 3 * Q_COLS + CP_COLS
FFN_HIDDEN = -(-(8 * D_MODEL) // (3 * 256)) * 256
EPS = 1e-6

V7X_LANES = 128
V7X_MXU_COLS = 256
V7X_BF16_SUBLANES = 16
V7X_VMEM_LIMIT_BYTES = 56 * 1024 * 1024

MOD_ROWS = 40
ONES_ROWS = V7X_BF16_SUBLANES
HALO = V7X_BF16_SUBLANES
FFN_CHUNK = V7X_MXU_COLS
N_FFN_CHUNKS = FFN_HIDDEN // FFN_CHUNK
ADA_COL_BLOCK = 1536
Q_PRESCALE = ATTN_SCALE * math.log2(math.e)

F32 = jnp.float32
BF16 = jnp.bfloat16


class Tiles(NamedTuple):
    tm_lat: int
    tm_ctx: int
    tq_lat: int
    tq_ctx: int


def _choose_tiles(seq_len, ctx_len):
    return Tiles(tm_lat=min(512, seq_len), tm_ctx=min(512, ctx_len),
                 tq_lat=min(V7X_MXU_COLS, seq_len), tq_ctx=min(V7X_MXU_COLS, ctx_len))


def _params(grid_rank=2):
    return pltpu.CompilerParams(dimension_semantics=("arbitrary",) * grid_rank,
                                vmem_limit_bytes=V7X_VMEM_LIMIT_BYTES)


def _dot(a, b):
    return jnp.dot(a, b, preferred_element_type=F32)


def _rms(x):
    return x * lax.rsqrt(jnp.mean(x * x, axis=-1, keepdims=True) + EPS)


def _layer_block(shape, layer, **kwargs):
    return pl.BlockSpec((None,) + tuple(shape), lambda bi, i: (layer,) + (0,) * len(shape), **kwargs)


def _mod_block(layer, which, mod_row):
    row = (lambda bi: bi) if mod_row is None else (lambda bi: mod_row)
    return pl.BlockSpec((None, None, None, 1, D_MODEL), lambda bi, i: (layer, row(bi), which, 0, 0))


def _ada_kernel(cc_ref, w_ref, b_ref, out_ref):
    cc = cc_ref[...]
    s = cc * jax.nn.sigmoid(cc)
    out_ref[0] = _dot(s.astype(BF16), w_ref[0].astype(BF16)) + b_ref[0]


def _ada_rows(cc, w_ada, b_ada):
    n_col = N_MOD * D_MODEL
    return pl.pallas_call(
        _ada_kernel,
        grid=(DEPTH, n_col // ADA_COL_BLOCK),
        in_specs=[
            pl.BlockSpec((MOD_ROWS, D_MODEL), lambda l, j: (0, 0)),
            pl.BlockSpec((1, D_MODEL, ADA_COL_BLOCK), lambda l, j: (l, 0, j)),
            pl.BlockSpec((1, 1, ADA_COL_BLOCK), lambda l, j: (l, 0, j)),
        ],
        out_specs=pl.BlockSpec((1, MOD_ROWS, ADA_COL_BLOCK), lambda l, j: (l, 0, j)),
        out_shape=jax.ShapeDtypeStruct((DEPTH, MOD_ROWS, n_col), F32),
        compiler_params=_params(),
        name="ada_rows",
    )(cc, w_ada, b_ada.reshape(DEPTH, 1, n_col))


def _rope_tables(length):
    pos = jnp.arange(length)
    row = (pos // GRID_W).astype(F32)
    col = (pos % GRID_W).astype(F32)
    inv = 1.0 / (ROPE_THETA ** (jnp.arange(ROPE_HALF, dtype=F32) / ROPE_HALF))
    ang_r = row[:, None] * inv[None, :]
    ang_c = col[:, None] * inv[None, :]
    ang = jnp.concatenate([ang_r, ang_r, ang_c, ang_c], axis=-1)
    cos = jnp.tile(jnp.cos(ang).astype(F32), (1, HEAD_COLS // ATTN_QK_DIM))
    sin = jnp.tile(jnp.sin(ang).astype(F32), (1, HEAD_COLS // ATTN_QK_DIM))
    first_half = (jnp.arange(HEAD_COLS) % (2 * ROPE_HALF)) < ROPE_HALF
    sin_from_upper = jnp.where(first_half, -sin, 0.0)
    sin_from_lower = jnp.where(first_half, 0.0, sin)
    return cos, sin_from_upper, sin_from_lower


def _inproj_kernel(*refs, rope, kv_only):
    x_ref, sh_ref, sc_ref, g_ref, w_ref = refs[:5]
    refs = refs[5:]
    if rope:
        cos_ref, sup_ref, slo_ref = refs[:3]
        refs = refs[3:]
    if kv_only:
        k_ref, vt_ref = refs
    else:
        q_ref, k_ref, vt_ref, cp_ref = refs

    h = (_rms(x_ref[0]) * g_ref[...]) * (1.0 + sc_ref[...]) + sh_ref[...]
    hb = h.astype(BF16)

    def rotate(t, scale):
        if not rope:
            return t if scale == 1.0 else t * scale
        out = []
        for j in range(Q_COLS // HEAD_COLS):
            tj = t[:, j * HEAD_COLS:(j + 1) * HEAD_COLS]
            rj = (tj * cos_ref[...]
                  + pltpu.roll(tj, HEAD_COLS - ROPE_HALF, 1) * sup_ref[...]
                  + pltpu.roll(tj, ROPE_HALF, 1) * slo_ref[...])
            out.append(rj if scale == 1.0 else rj * scale)
        return jnp.concatenate(out, axis=1)

    if not kv_only:
        q = _dot(hb, w_ref[:, 0:Q_COLS])
        q_ref[0] = rotate(q, Q_PRESCALE).astype(BF16)
    k = _dot(hb, w_ref[:, Q_COLS:2 * Q_COLS])
    k_ref[0] = rotate(k, 1.0).astype(BF16)
    v = _dot(hb, w_ref[:, 2 * Q_COLS:3 * Q_COLS])
    vt_ref[0] = v.T.astype(BF16)
    if not kv_only:
        cp_ref[0] = _dot(hb, w_ref[:, 3 * Q_COLS:IN_COLS]).astype(BF16)


def _inproj(x, mod, mod_row, g_norm1, w_in, tables, *, layer, tm, kv_only=False):
    b, length, _ = x.shape
    rope = tables is not None
    in_specs = [
        pl.BlockSpec((1, tm, D_MODEL), lambda bi, i: (bi, i, 0)),
        _mod_block(layer, 0, mod_row),
        _mod_block(layer, 1, mod_row),
        _layer_block((1, D_MODEL), layer),
        _layer_block((D_MODEL, IN_COLS), layer, pipeline_mode=pl.Buffered(1)),
    ]
    args = [x, mod, mod, g_norm1, w_in]
    if rope:
        in_specs += [pl.BlockSpec((tm, HEAD_COLS), lambda bi, i: (i, 0))] * 3
        args += list(tables)
    row_spec = lambda cols: pl.BlockSpec((1, tm, cols), lambda bi, i: (bi, i, 0))
    row_shape = lambda cols: jax.ShapeDtypeStruct((b, length, cols), BF16)
    vt_spec = pl.BlockSpec((1, V_COLS, tm), lambda bi, i: (bi, 0, i))
    vt_shape = jax.ShapeDtypeStruct((b, V_COLS, length), BF16)
    if kv_only:
        out_specs = [row_spec(Q_COLS), vt_spec]
        out_shape = [row_shape(Q_COLS), vt_shape]
    else:
        out_specs = [row_spec(Q_COLS), row_spec(Q_COLS), vt_spec, row_spec(CP_COLS)]
        out_shape = [row_shape(Q_COLS), row_shape(Q_COLS), vt_shape, row_shape(CP_COLS)]
    return pl.pallas_call(
        functools.partial(_inproj_kernel, rope=rope, kv_only=kv_only),
        grid=(b, length // tm),
        in_specs=in_specs,
        out_specs=out_specs,
        out_shape=out_shape,
        compiler_params=_params(),
        name="inproj_lat" if rope else "inproj_ctx",
    )(*args)


def _attn_kernel(*refs, key_lens, lam_init, tq, heads):
    n_src = len(key_lens)
    lam_ref, g_ref, q_ref = refs[:3]
    k_refs = refs[3:3 + n_src]
    vt_refs = refs[3 + n_src:3 + 2 * n_src]
    o_ref = refs[3 + 2 * n_src]
    k_all, vt_all = refs[4 + 2 * n_src:6 + 2 * n_src]
    scratch = refs[6 + 2 * n_src:]
    s_ref = (scratch[0:2], scratch[2:4])
    p_ref = scratch[4:6]

    lq = lam_ref[...]
    lam = (jnp.exp(jnp.sum(lq[0:1] * lq[1:2], axis=1, keepdims=True))
           - jnp.exp(jnp.sum(lq[2:3] * lq[3:4], axis=1, keepdims=True)) + lam_init)
    lane = lax.broadcasted_iota(jnp.int32, (tq, HEAD_COLS), 1)
    n_q = q_ref.shape[1] // tq
    lk_total = sum(key_lens)
    vt_all[ATTN_V_DIM:, :] = jnp.ones((ONES_ROWS, lk_total), BF16)

    def one_head(head):
        cols = slice(head * HEAD_COLS, (head + 1) * HEAD_COLS)

        off = 0
        for k_ref, vt_ref, lk in zip(k_refs, vt_refs, key_lens):
            k_all[off:off + lk, :] = k_ref[0, :, cols]
            vt_all[0:ATTN_V_DIM, off:off + lk] = vt_ref[0, cols, :]
            off += lk

        def scores(i, slot):
            q = q_ref[0, pl.ds(pl.multiple_of(i * tq, tq), tq), cols]
            zero = jnp.zeros_like(q)
            maxima = []
            for c, keep in enumerate((lane < ATTN_QK_DIM, lane >= ATTN_QK_DIM)):
                s = lax.dot_general(k_all[...], jnp.where(keep, q, zero), (((1,), (1,)), ((), ())),
                                    preferred_element_type=F32)
                s_ref[slot][c][...] = s
                maxima.append(jnp.max(s, axis=0, keepdims=True))
            return tuple(maxima)

        def finish(i, slot, maxima):
            outs = []
            for c in range(2):
                p_ref[c][...] = jnp.exp2(s_ref[slot][c][...] - maxima[c]).astype(BF16)
                av = _dot(vt_all[...], p_ref[c][...])
                outs.append(av[0:ATTN_V_DIM] / av[ATTN_V_DIM:ATTN_V_DIM + 1])
            o_t = outs[0] - lam * outs[1]
            y = o_t * lax.rsqrt(jnp.mean(o_t * o_t, axis=0, keepdims=True) + EPS)
            o = (y.T * g_ref[...]) * (1.0 - lam_init)
            o_ref[0, pl.ds(pl.multiple_of(i * tq, tq), tq), cols] = o.astype(BF16)

        maxima = scores(0, 0)
        if n_q > 1:
            assert n_q % 2 == 0

            def pair(j, maxima0):
                maxima1 = scores(2 * j + 1, 1)
                finish(2 * j, 0, maxima0)
                maxima2 = scores(2 * j + 2, 0)
                finish(2 * j + 1, 1, maxima1)
                return maxima2

            maxima = lax.fori_loop(0, n_q // 2 - 1, pair, maxima)
            maxima_last = scores(n_q - 1, 1)
            finish(n_q - 2, 0, maxima)
            finish(n_q - 1, 1, maxima_last)
        else:
            finish(0, 0, maxima)

    for head in range(heads):
        one_head(head)


def _attention(lam_rows, g_subln, q, ks, vts, *, layer, lam_init, tq, heads):
    b, lq, _ = q.shape
    key_lens = tuple(k.shape[1] for k in ks)
    lk_total = sum(key_lens)
    cols = heads * HEAD_COLS
    head_rows = lambda length: pl.BlockSpec((1, length, cols), lambda bi, h: (bi, 0, h))
    in_specs = [
        _layer_block((4, ATTN_QK_DIM), layer),
        _layer_block((1, ATTN_V_DIM), layer),
        head_rows(lq),
    ]
    in_specs += [head_rows(lk) for lk in key_lens]
    in_specs += [pl.BlockSpec((1, cols, lk), lambda bi, h: (bi, h, 0)) for lk in key_lens]
    return pl.pallas_call(
        functools.partial(_attn_kernel, key_lens=key_lens, lam_init=lam_init, tq=tq, heads=heads),
        grid=(b, ATTN_HEADS // heads),
        in_specs=in_specs,
        out_specs=head_rows(lq),
        out_shape=jax.ShapeDtypeStruct((b, lq, V_COLS), BF16),
        scratch_shapes=[pltpu.VMEM((lk_total, HEAD_COLS), BF16), pltpu.VMEM((ATTN_V_DIM + ONES_ROWS, lk_total), BF16),
                        *[pltpu.VMEM((lk_total, tq), F32)] * 4, *[pltpu.VMEM((lk_total, tq), BF16)] * 2],
        compiler_params=_params(),
        name="diff_attn_lat" if len(ks) == 2 else "diff_attn_ctx",
    )(lam_rows, g_subln, q, *ks, *vts)


def _mixffn_kernel(x_ref, o_ref, cp_prev_ref, cp_ref, cp_next_ref, gt1_ref, sh2_ref, sc2_ref, gt2_ref,
                   g2_ref, wconv_ref, wpool_ref, spool_ref, wout_ref, wgu_ref, wd_ref, gfin_ref,
                   out_ref, acc_ref, *, length, tm, final):
    i = pl.program_id(1)
    n_tiles = length // tm
    rows = tm + 2 * HALO

    prev = jnp.where(i > 0, cp_prev_ref[0].astype(F32), 0.0)
    nxt = jnp.where(i < n_tiles - 1, cp_next_ref[0].astype(F32), 0.0)
    cur = cp_ref[0].astype(F32)
    ext = jnp.concatenate([prev, cur, nxt], axis=0)
    core = lambda a: a[HALO:HALO + tm]

    def shifted(a, d):
        return pltpu.roll(a, d % rows, 0)

    u = ext[:, CONV_WIDTH:2 * CONV_WIDTH] * ext[:, 2 * CONV_WIDTH:3 * CONV_WIDTH]
    wc = wconv_ref[...]
    y = shifted(u, 1) * wc[0:1] + u * wc[1:2] + shifted(u, -1) * wc[2:3]
    conv = cur[:, 0:CONV_WIDTH] * core(y)

    t = i * tm + lax.broadcasted_iota(jnp.int32, (tm, V7X_LANES), 0)
    lane = lax.broadcasted_iota(jnp.int32, (tm, V7X_LANES), 1)
    pooled = []
    for tile in range(POOL_WIDTH // V7X_LANES):
        p = ext[:, 3 * CONV_WIDTH + tile * V7X_LANES:3 * CONV_WIDTH + (tile + 1) * V7X_LANES]
        sums = {2: p + shifted(p, 1)}
        for w in (4, 8, 16):
            sums[w] = shifted(sums[w // 2], w // 4) + shifted(sums[w // 2], -(w // 4))
        w_lo, w_hi = POOL_WINDOWS[2 * tile], POOL_WINDOWS[2 * tile + 1]
        lower_group = lane < POOL_GROUP_DIM
        half = jnp.where(lower_group, w_lo // 2, w_hi // 2)
        cnt = jnp.minimum(t + half - 1, length - 1) - jnp.maximum(t - half, 0) + 1
        win_sum = jnp.where(lower_group, core(sums[w_lo]), core(sums[w_hi]))
        pooled.append(win_sum / cnt.astype(F32) - core(p))
    pooled = jnp.concatenate(pooled, axis=1).astype(BF16)
    pool_y = _dot(pooled, wpool_ref[...]) * spool_ref[...]

    wo = wout_ref
    mix = (_dot(o_ref[0], wo[0:V_COLS, :])
           + _dot(conv.astype(BF16), wo[V_COLS:V_COLS + CONV_WIDTH, :])
           + _dot(pool_y.astype(BF16), wo[V_COLS + CONV_WIDTH:D_MODEL, :]))
    x1 = x_ref[0] + gt1_ref[...] * mix
    out_ref[0] = x1
    h2 = ((_rms(x1) * g2_ref[...]) * (1.0 + sc2_ref[...]) + sh2_ref[...]).astype(BF16)

    for c in range(N_FFN_CHUNKS):
        gate = _dot(h2, wgu_ref[:, c * FFN_CHUNK:(c + 1) * FFN_CHUNK])
        up = _dot(h2, wgu_ref[:, FFN_HIDDEN + c * FFN_CHUNK:FFN_HIDDEN + (c + 1) * FFN_CHUNK])
        act = (gate * jax.nn.sigmoid(gate) * up).astype(BF16)
        part = _dot(act, wd_ref[c * FFN_CHUNK:(c + 1) * FFN_CHUNK, :])
        if c == 0:
            acc_ref[...] = part
        else:
            acc_ref[...] += part

    x2 = out_ref[0] + gt2_ref[...] * acc_ref[...]
    if final:
        x2 = _rms(x2) * gfin_ref[...]
    out_ref[0] = x2


def _mixffn(x, o, cp, mod, mod_row, g_norm2, w_conv, w_pool_bd, s_pool, w_out, w_gate_up, w_down, g_final,
            *, layer, tm, final):
    b, length, _ = x.shape
    n_tiles = length // tm
    halo_per_tile = tm // HALO
    n_halo_blocks = length // HALO
    weight = functools.partial(_layer_block, layer=layer, pipeline_mode=pl.Buffered(1))
    in_specs = [
        pl.BlockSpec((1, tm, D_MODEL), lambda bi, i: (bi, i, 0)),
        pl.BlockSpec((1, tm, V_COLS), lambda bi, i: (bi, i, 0)),
        pl.BlockSpec((1, HALO, CP_COLS), lambda bi, i: (bi, jnp.maximum(i * halo_per_tile - 1, 0), 0)),
        pl.BlockSpec((1, tm, CP_COLS), lambda bi, i: (bi, i, 0)),
        pl.BlockSpec((1, HALO, CP_COLS),
                     lambda bi, i: (bi, jnp.minimum((i + 1) * halo_per_tile, n_halo_blocks - 1), 0)),
    ] + [_mod_block(layer, which, mod_row) for which in (2, 3, 4, 5)] + [
        _layer_block((1, D_MODEL), layer),
        _layer_block((3, CONV_WIDTH), layer),
        weight((POOL_WIDTH, POOL_WIDTH)),
        _layer_block((1, POOL_WIDTH), layer),
        weight((D_MODEL, D_MODEL)),
        weight((D_MODEL, 2 * FFN_HIDDEN)),
        weight((FFN_HIDDEN, D_MODEL)),
        pl.BlockSpec((1, D_MODEL), lambda bi, i: (0, 0)),
    ]
    return pl.pallas_call(
        functools.partial(_mixffn_kernel, length=length, tm=tm, final=final),
        grid=(b, n_tiles),
        in_specs=in_specs,
        out_specs=pl.BlockSpec((1, tm, D_MODEL), lambda bi, i: (bi, i, 0)),
        out_shape=jax.ShapeDtypeStruct((b, length, D_MODEL), F32),
        scratch_shapes=[pltpu.VMEM((tm, D_MODEL), F32)],
        compiler_params=_params(),
        name="mix_ffn_lat" if mod_row is None else "mix_ffn_ctx",
    )(x, o, cp, cp, cp, mod, mod, mod, mod, g_norm2, w_conv, w_pool_bd, s_pool, w_out, w_gate_up, w_down, g_final)


@jax.jit
def _forward(x, c, ctx, c_ctx, w_ada, b_ada, g_norm1, w_in, lam_q1, lam_k1, lam_q2, lam_k2,
             g_subln, w_conv, w_pool, s_pool, w_out, g_norm2, w_gate_up, w_down, g_final):
    batch, seq_len, _ = x.shape
    ctx_len = ctx.shape[1]
    ctx_row = batch
    assert batch < MOD_ROWS
    tiles = _choose_tiles(seq_len, ctx_len)

    cc = jnp.concatenate([c, c_ctx[None, :], jnp.zeros((MOD_ROWS - batch - 1, D_MODEL), F32)], axis=0)
    mod = _ada_rows(cc, w_ada, b_ada).reshape(DEPTH, MOD_ROWS, N_MOD, 1, D_MODEL)

    w_in_b = w_in.astype(BF16)
    w_out_b = w_out.astype(BF16)
    w_gu_b = w_gate_up.astype(BF16)
    w_down_b = w_down.astype(BF16)
    eye = jnp.eye(POOL_GROUPS, dtype=F32)
    w_pool_bd = (w_pool[:, :, :, None, :] * eye[None, :, None, :, None]).reshape(
        DEPTH, POOL_WIDTH, POOL_WIDTH).astype(BF16)
    lam_rows = jnp.stack([lam_q1, lam_k1, lam_q2, lam_k2], axis=1)
    g1 = g_norm1[:, None, :]
    g2 = g_norm2[:, None, :]
    gs = g_subln[:, None, :]
    sp = s_pool[:, None, :]
    gf = g_final[None, :]
    tables = _rope_tables(seq_len)

    x_lat, x_ctx = x, ctx
    for l in range(DEPTH):
        last = l == DEPTH - 1
        lam_init = 0.8 - 0.6 * math.exp(-0.3 * l)
        ctx_proj = _inproj(x_ctx, mod, ctx_row, g1, w_in_b, None, layer=l, tm=tiles.tm_ctx, kv_only=last)
        q_l, k_l, vt_l, cp_l = _inproj(x_lat, mod, None, g1, w_in_b, tables, layer=l, tm=tiles.tm_lat)
        k_c, vt_c = ctx_proj[-3:-1] if not last else ctx_proj
        o_l = _attention(lam_rows, gs, q_l, [k_c, k_l], [vt_c, vt_l], layer=l, lam_init=lam_init,
                         tq=tiles.tq_lat, heads=1)
        mix_args = (g2, w_conv, w_pool_bd, sp, w_out_b, w_gu_b, w_down_b, gf)
        x_lat = _mixffn(x_lat, o_l, cp_l, mod, None, *mix_args, layer=l, tm=tiles.tm_lat, final=last)
        if not last:
            q_c, _, _, cp_c = ctx_proj
            o_c = _attention(lam_rows, gs, q_c, [k_c], [vt_c], layer=l, lam_init=lam_init,
                             tq=tiles.tq_ctx, heads=ATTN_HEADS)
            x_ctx = _mixffn(x_ctx, o_c, cp_c, mod, ctx_row, *mix_args, layer=l, tm=tiles.tm_ctx, final=False)
    return x_lat


def kernel(x, c, ctx, c_ctx, w_ada, b_ada, g_norm1, w_in, lam_q1, lam_k1, lam_q2, lam_k2, g_subln, w_conv,
           w_pool, s_pool, w_out, g_norm2, w_gate_up, w_down, g_final):
    return _forward(x, c, ctx, c_ctx, w_ada, b_ada, g_norm1, w_in, lam_q1, lam_k1, lam_q2, lam_k2, g_subln,
                    w_conv, w_pool, s_pool, w_out, g_norm2, w_gate_up, w_down, g_final)
```

```python
import functools
import math
from typing import NamedTuple

import jax
import jax.numpy as jnp
from jax import lax
from jax.experimental import pallas as pl
from jax.experimental.pallas import tpu as pltpu

D_MODEL = 1024
DEPTH = 4
GRID_W = 64
N_MOD = 6
ATTN_HEADS = 4
ATTN_QK_DIM = 64
ATTN_V_DIM = 2 * ATTN_QK_DIM
HEAD_COLS = 2 * ATTN_QK_DIM
Q_COLS = ATTN_HEADS * HEAD_COLS
V_COLS = ATTN_HEADS * ATTN_V_DIM
ATTN_SCALE = ATTN_QK_DIM ** -0.5
ROPE_THETA = 10000.0
ROPE_HALF = ATTN_QK_DIM // 4
CONV_WIDTH = D_MODEL // 4
POOL_WINDOWS = (2, 4, 8, 16)
POOL_GROUPS = len(POOL_WINDOWS)
POOL_WIDTH = D_MODEL - V_COLS - CONV_WIDTH
POOL_GROUP_DIM = POOL_WIDTH // POOL_GROUPS
CP_COLS = 3 * CONV_WIDTH + POOL_WIDTH
IN_COLS = 3 * Q_COLS + CP_COLS
FFN_HIDDEN = -(-(8 * D_MODEL) // (3 * 256)) * 256
EPS = 1e-6

V7X_LANES = 128
V7X_MXU_COLS = 256
V7X_BF16_SUBLANES = 16
V7X_VMEM_LIMIT_BYTES = 56 * 1024 * 1024

MOD_ROWS = 40
ONES_ROWS = V7X_BF16_SUBLANES
HALO = V7X_BF16_SUBLANES
FFN_CHUNK = V7X_MXU_COLS
N_FFN_CHUNKS = FFN_HIDDEN // FFN_CHUNK
ADA_COL_BLOCK = 1536
Q_PRESCALE = ATTN_SCALE * math.log2(math.e)

F32 = jnp.float32
BF16 = jnp.bfloat16


class Tiles(NamedTuple):
    tm_lat: int
    tm_ctx: int
    tq_lat: int
    tq_ctx: int


def _choose_tiles(seq_len, ctx_len):
    return Tiles(tm_lat=min(512, seq_len), tm_ctx=min(512, ctx_len),
                 tq_lat=min(V7X_MXU_COLS, seq_len), tq_ctx=min(V7X_MXU_COLS, ctx_len))


def _params(grid_rank=2):
    return pltpu.CompilerParams(dimension_semantics=("arbitrary",) * grid_rank,
                                vmem_limit_bytes=V7X_VMEM_LIMIT_BYTES)


def _dot(a, b):
    return jnp.dot(a, b, preferred_element_type=F32)


def _rms(x):
    return x * lax.rsqrt(jnp.mean(x * x, axis=-1, keepdims=True) + EPS)


def _layer_block(shape, layer, **kwargs):
    return pl.BlockSpec((None,) + tuple(shape), lambda bi, i: (layer,) + (0,) * len(shape), **kwargs)


def _mod_block(layer, which, mod_row):
    row = (lambda bi: bi) if mod_row is None else (lambda bi: mod_row)
    return pl.BlockSpec((None, None, None, 1, D_MODEL), lambda bi, i: (layer, row(bi), which, 0, 0))


def _ada_kernel(cc_ref, w_ref, b_ref, out_ref):
    cc = cc_ref[...]
    s = cc * jax.nn.sigmoid(cc)
    out_ref[0] = _dot(s.astype(BF16), w_ref[0].astype(BF16)) + b_ref[0]


def _ada_rows(cc, w_ada, b_ada):
    n_col = N_MOD * D_MODEL
    return pl.pallas_call(
        _ada_kernel,
        grid=(DEPTH, n_col // ADA_COL_BLOCK),
        in_specs=[
            pl.BlockSpec((MOD_ROWS, D_MODEL), lambda l, j: (0, 0)),
            pl.BlockSpec((1, D_MODEL, ADA_COL_BLOCK), lambda l, j: (l, 0, j)),
            pl.BlockSpec((1, 1, ADA_COL_BLOCK), lambda l, j: (l, 0, j)),
        ],
        out_specs=pl.BlockSpec((1, MOD_ROWS, ADA_COL_BLOCK), lambda l, j: (l, 0, j)),
        out_shape=jax.ShapeDtypeStruct((DEPTH, MOD_ROWS, n_col), F32),
        compiler_params=_params(),
        name="ada_rows",
    )(cc, w_ada, b_ada.reshape(DEPTH, 1, n_col))


def _rope_tables(length):
    pos = jnp.arange(length)
    row = (pos // GRID_W).astype(F32)
    col = (pos % GRID_W).astype(F32)
    inv = 1.0 / (ROPE_THETA ** (jnp.arange(ROPE_HALF, dtype=F32) / ROPE_HALF))
    ang_r = row[:, None] * inv[None, :]
    ang_c = col[:, None] * inv[None, :]
    ang = jnp.concatenate([ang_r, ang_r, ang_c, ang_c], axis=-1)
    cos = jnp.tile(jnp.cos(ang).astype(F32), (1, HEAD_COLS // ATTN_QK_DIM))
    sin = jnp.tile(jnp.sin(ang).astype(F32), (1, HEAD_COLS // ATTN_QK_DIM))
    first_half = (jnp.arange(HEAD_COLS) % (2 * ROPE_HALF)) < ROPE_HALF
    sin_from_upper = jnp.where(first_half, -sin, 0.0)
    sin_from_lower = jnp.where(first_half, 0.0, sin)
    return cos, sin_from_upper, sin_from_lower


def _inproj_kernel(*refs, rope, kv_only):
    x_ref, sh_ref, sc_ref, g_ref, w_ref = refs[:5]
    refs = refs[5:]
    if rope:
        cos_ref, sup_ref, slo_ref = refs[:3]
        refs = refs[3:]
    if kv_only:
        k_ref, vt_ref = refs
    else:
        q_ref, k_ref, vt_ref, cp_ref = refs

    h = (_rms(x_ref[0]) * g_ref[...]) * (1.0 + sc_ref[...]) + sh_ref[...]
    hb = h.astype(BF16)

    def rotate(t, scale):
        if not rope:
            return t if scale == 1.0 else t * scale
        out = []
        for j in range(Q_COLS // HEAD_COLS):
            tj = t[:, j * HEAD_COLS:(j + 1) * HEAD_COLS]
            rj = (tj * cos_ref[...]
                  + pltpu.roll(tj, HEAD_COLS - ROPE_HALF, 1) * sup_ref[...]
                  + pltpu.roll(tj, ROPE_HALF, 1) * slo_ref[...])
            out.append(rj if scale == 1.0 else rj * scale)
        return jnp.concatenate(out, axis=1)

    if not kv_only:
        q = _dot(hb, w_ref[:, 0:Q_COLS])
        q_ref[0] = rotate(q, Q_PRESCALE).astype(BF16)
    k = _dot(hb, w_ref[:, Q_COLS:2 * Q_COLS])
    k_ref[0] = rotate(k, 1.0).astype(BF16)
    v = _dot(hb, w_ref[:, 2 * Q_COLS:3 * Q_COLS])
    vt_ref[0] = v.T.astype(BF16)
    if not kv_only:
        cp_ref[0] = _dot(hb, w_ref[:, 3 * Q_COLS:IN_COLS]).astype(BF16)


def _inproj(x, mod, mod_row, g_norm1, w_in, tables, *, layer, tm, kv_only=False):
    b, length, _ = x.shape
    rope = tables is not None
    in_specs = [
        pl.BlockSpec((1, tm, D_MODEL), lambda bi, i: (bi, i, 0)),
        _mod_block(layer, 0, mod_row),
        _mod_block(layer, 1, mod_row),
        _layer_block((1, D_MODEL), layer),
        _layer_block((D_MODEL, IN_COLS), layer, pipeline_mode=pl.Buffered(1)),
    ]
    args = [x, mod, mod, g_norm1, w_in]
    if rope:
        in_specs += [pl.BlockSpec((tm, HEAD_COLS), lambda bi, i: (i, 0))] * 3
        args += list(tables)
    row_spec = lambda cols: pl.BlockSpec((1, tm, cols), lambda bi, i: (bi, i, 0))
    row_shape = lambda cols: jax.ShapeDtypeStruct((b, length, cols), BF16)
    vt_spec = pl.BlockSpec((1, V_COLS, tm), lambda bi, i: (bi, 0, i))
    vt_shape = jax.ShapeDtypeStruct((b, V_COLS, length), BF16)
    if kv_only:
        out_specs = [row_spec(Q_COLS), vt_spec]
        out_shape = [row_shape(Q_COLS), vt_shape]
    else:
        out_specs = [row_spec(Q_COLS), row_spec(Q_COLS), vt_spec, row_spec(CP_COLS)]
        out_shape = [row_shape(Q_COLS), row_shape(Q_COLS), vt_shape, row_shape(CP_COLS)]
    return pl.pallas_call(
        functools.partial(_inproj_kernel, rope=rope, kv_only=kv_only),
        grid=(b, length // tm),
        in_specs=in_specs,
        out_specs=out_specs,
        out_shape=out_shape,
        compiler_params=_params(),
        name="inproj_lat" if rope else "inproj_ctx",
    )(*args)


def _attn_kernel(*refs, key_lens, lam_init, tq, heads):
    n_src = len(key_lens)
    lam_ref, g_ref, q_ref = refs[:3]
    k_refs = refs[3:3 + n_src]
    vt_refs = refs[3 + n_src:3 + 2 * n_src]
    o_ref = refs[3 + 2 * n_src]
    k_all, vt_all = refs[4 + 2 * n_src:6 + 2 * n_src]
    scratch = refs[6 + 2 * n_src:]
    s_ref = (scratch[0:2], scratch[2:4])
    p_ref = (scratch[4:6], scratch[6:8])

    lq = lam_ref[...]
    lam = (jnp.exp(jnp.sum(lq[0:1] * lq[1:2], axis=1, keepdims=True))
           - jnp.exp(jnp.sum(lq[2:3] * lq[3:4], axis=1, keepdims=True)) + lam_init)
    lane = lax.broadcasted_iota(jnp.int32, (tq, HEAD_COLS), 1)
    n_q = q_ref.shape[1] // tq
    lk_total = sum(key_lens)
    vt_all[ATTN_V_DIM:, :] = jnp.ones((ONES_ROWS, lk_total), BF16)

    def one_head(head):
        cols = slice(head * HEAD_COLS, (head + 1) * HEAD_COLS)

        off = 0
        for k_ref, vt_ref, lk in zip(k_refs, vt_refs, key_lens):
            k_all[off:off + lk, :] = k_ref[0, :, cols]
            vt_all[0:ATTN_V_DIM, off:off + lk] = vt_ref[0, cols, :]
            off += lk

        def scores(i, slot):
            q = q_ref[0, pl.ds(pl.multiple_of(i * tq, tq), tq), cols]
            zero = jnp.zeros_like(q)
            maxima = []
            for c, keep in enumerate((lane < ATTN_QK_DIM, lane >= ATTN_QK_DIM)):
                s = lax.dot_general(k_all[...], jnp.where(keep, q, zero), (((1,), (1,)), ((), ())),
                                    preferred_element_type=F32)
                s_ref[slot][c][...] = s
                maxima.append(jnp.max(s, axis=0, keepdims=True))
            return tuple(maxima)

        def softmax(slot, maxima):
            for c in range(2):
                p_ref[slot][c][...] = jnp.exp2(s_ref[slot][c][...] - maxima[c]).astype(BF16)

        def values(i, slot):
            outs = []
            for c in range(2):
                av = _dot(vt_all[...], p_ref[slot][c][...])
                outs.append(av[0:ATTN_V_DIM] / av[ATTN_V_DIM:ATTN_V_DIM + 1])
            o_t = outs[0] - lam * outs[1]
            y = o_t * lax.rsqrt(jnp.mean(o_t * o_t, axis=0, keepdims=True) + EPS)
            o = (y.T * g_ref[...]) * (1.0 - lam_init)
            o_ref[0, pl.ds(pl.multiple_of(i * tq, tq), tq), cols] = o.astype(BF16)

        if n_q == 1:
            softmax(0, scores(0, 0))
            values(0, 0)
        else:
            assert n_q % 2 == 0
            max_even = scores(0, 0)
            max_odd = scores(1, 1)
            softmax(0, max_even)

            def pair(j, max_odd):
                max_even = scores(2 * j + 2, 0)
                softmax(1, max_odd)
                values(2 * j, 0)
                max_odd = scores(2 * j + 3, 1)
                softmax(0, max_even)
                values(2 * j + 1, 1)
                return max_odd

            max_odd = lax.fori_loop(0, n_q // 2 - 1, pair, max_odd)
            softmax(1, max_odd)
            values(n_q - 2, 0)
            values(n_q - 1, 1)

    for head in range(heads):
        one_head(head)


def _attention(lam_rows, g_subln, q, ks, vts, *, layer, lam_init, tq, heads):
    b, lq, _ = q.shape
    key_lens = tuple(k.shape[1] for k in ks)
    lk_total = sum(key_lens)
    cols = heads * HEAD_COLS
    head_rows = lambda length: pl.BlockSpec((1, length, cols), lambda bi, h: (bi, 0, h))
    in_specs = [
        _layer_block((4, ATTN_QK_DIM), layer),
        _layer_block((1, ATTN_V_DIM), layer),
        head_rows(lq),
    ]
    in_specs += [head_rows(lk) for lk in key_lens]
    in_specs += [pl.BlockSpec((1, cols, lk), lambda bi, h: (bi, h, 0)) for lk in key_lens]
    return pl.pallas_call(
        functools.partial(_attn_kernel, key_lens=key_lens, lam_init=lam_init, tq=tq, heads=heads),
        grid=(b, ATTN_HEADS // heads),
        in_specs=in_specs,
        out_specs=head_rows(lq),
        out_shape=jax.ShapeDtypeStruct((b, lq, V_COLS), BF16),
        scratch_shapes=[pltpu.VMEM((lk_total, HEAD_COLS), BF16), pltpu.VMEM((ATTN_V_DIM + ONES_ROWS, lk_total), BF16),
                        *[pltpu.VMEM((lk_total, tq), F32)] * 4, *[pltpu.VMEM((lk_total, tq), BF16)] * 4],
        compiler_params=_params(),
        name="diff_attn_lat" if len(ks) == 2 else "diff_attn_ctx",
    )(lam_rows, g_subln, q, *ks, *vts)


def _mixffn_kernel(x_ref, o_ref, cp_prev_ref, cp_ref, cp_next_ref, gt1_ref, sh2_ref, sc2_ref, gt2_ref,
                   g2_ref, wconv_ref, wpool_ref, spool_ref, wout_ref, wgu_ref, wd_ref, gfin_ref,
                   out_ref, acc_ref, *, length, tm, final):
    i = pl.program_id(1)
    n_tiles = length // tm
    rows = tm + 2 * HALO

    prev = jnp.where(i > 0, cp_prev_ref[0].astype(F32), 0.0)
    nxt = jnp.where(i < n_tiles - 1, cp_next_ref[0].astype(F32), 0.0)
    cur = cp_ref[0].astype(F32)
    ext = jnp.concatenate([prev, cur, nxt], axis=0)
    core = lambda a: a[HALO:HALO + tm]

    def shifted(a, d):
        return pltpu.roll(a, d % rows, 0)

    u = ext[:, CONV_WIDTH:2 * CONV_WIDTH] * ext[:, 2 * CONV_WIDTH:3 * CONV_WIDTH]
    wc = wconv_ref[...]
    y = shifted(u, 1) * wc[0:1] + u * wc[1:2] + shifted(u, -1) * wc[2:3]
    conv = cur[:, 0:CONV_WIDTH] * core(y)

    t = i * tm + lax.broadcasted_iota(jnp.int32, (tm, V7X_LANES), 0)
    lane = lax.broadcasted_iota(jnp.int32, (tm, V7X_LANES), 1)
    pooled = []
    for tile in range(POOL_WIDTH // V7X_LANES):
        p = ext[:, 3 * CONV_WIDTH + tile * V7X_LANES:3 * CONV_WIDTH + (tile + 1) * V7X_LANES]
        sums = {2: p + shifted(p, 1)}
        for w in (4, 8, 16):
            sums[w] = shifted(sums[w // 2], w // 4) + shifted(sums[w // 2], -(w // 4))
        w_lo, w_hi = POOL_WINDOWS[2 * tile], POOL_WINDOWS[2 * tile + 1]
        lower_group = lane < POOL_GROUP_DIM
        half = jnp.where(lower_group, w_lo // 2, w_hi // 2)
        cnt = jnp.minimum(t + half - 1, length - 1) - jnp.maximum(t - half, 0) + 1
        win_sum = jnp.where(lower_group, core(sums[w_lo]), core(sums[w_hi]))
        pooled.append(win_sum / cnt.astype(F32) - core(p))
    pooled = jnp.concatenate(pooled, axis=1).astype(BF16)
    pool_y = _dot(pooled, wpool_ref[...]) * spool_ref[...]

    wo = wout_ref
    mix = (_dot(o_ref[0], wo[0:V_COLS, :])
           + _dot(conv.astype(BF16), wo[V_COLS:V_COLS + CONV_WIDTH, :])
           + _dot(pool_y.astype(BF16), wo[V_COLS + CONV_WIDTH:D_MODEL, :]))
    x1 = x_ref[0] + gt1_ref[...] * mix
    out_ref[0] = x1
    h2 = ((_rms(x1) * g2_ref[...]) * (1.0 + sc2_ref[...]) + sh2_ref[...]).astype(BF16)

    for c in range(N_FFN_CHUNKS):
        gate = _dot(h2, wgu_ref[:, c * FFN_CHUNK:(c + 1) * FFN_CHUNK])
        up = _dot(h2, wgu_ref[:, FFN_HIDDEN + c * FFN_CHUNK:FFN_HIDDEN + (c + 1) * FFN_CHUNK])
        act = (gate * jax.nn.sigmoid(gate) * up).astype(BF16)
        part = _dot(act, wd_ref[c * FFN_CHUNK:(c + 1) * FFN_CHUNK, :])
        if c == 0:
            acc_ref[...] = part
        else:
            acc_ref[...] += part

    x2 = out_ref[0] + gt2_ref[...] * acc_ref[...]
    if final:
        x2 = _rms(x2) * gfin_ref[...]
    out_ref[0] = x2


def _mixffn(x, o, cp, mod, mod_row, g_norm2, w_conv, w_pool_bd, s_pool, w_out, w_gate_up, w_down, g_final,
            *, layer, tm, final):
    b, length, _ = x.shape
    n_tiles = length // tm
    halo_per_tile = tm // HALO
    n_halo_blocks = length // HALO
    weight = functools.partial(_layer_block, layer=layer, pipeline_mode=pl.Buffered(1))
    in_specs = [
        pl.BlockSpec((1, tm, D_MODEL), lambda bi, i: (bi, i, 0)),
        pl.BlockSpec((1, tm, V_COLS), lambda bi, i: (bi, i, 0)),
        pl.BlockSpec((1, HALO, CP_COLS), lambda bi, i: (bi, jnp.maximum(i * halo_per_tile - 1, 0), 0)),
        pl.BlockSpec((1, tm, CP_COLS), lambda bi, i: (bi, i, 0)),
        pl.BlockSpec((1, HALO, CP_COLS),
                     lambda bi, i: (bi, jnp.minimum((i + 1) * halo_per_tile, n_halo_blocks - 1), 0)),
    ] + [_mod_block(layer, which, mod_row) for which in (2, 3, 4, 5)] + [
        _layer_block((1, D_MODEL), layer),
        _layer_block((3, CONV_WIDTH), layer),
        weight((POOL_WIDTH, POOL_WIDTH)),
        _layer_block((1, POOL_WIDTH), layer),
        weight((D_MODEL, D_MODEL)),
        weight((D_MODEL, 2 * FFN_HIDDEN)),
        weight((FFN_HIDDEN, D_MODEL)),
        pl.BlockSpec((1, D_MODEL), lambda bi, i: (0, 0)),
    ]
    return pl.pallas_call(
        functools.partial(_mixffn_kernel, length=length, tm=tm, final=final),
        grid=(b, n_tiles),
        in_specs=in_specs,
        out_specs=pl.BlockSpec((1, tm, D_MODEL), lambda bi, i: (bi, i, 0)),
        out_shape=jax.ShapeDtypeStruct((b, length, D_MODEL), F32),
        scratch_shapes=[pltpu.VMEM((tm, D_MODEL), F32)],
        compiler_params=_params(),
        name="mix_ffn_lat" if mod_row is None else "mix_ffn_ctx",
    )(x, o, cp, cp, cp, mod, mod, mod, mod, g_norm2, w_conv, w_pool_bd, s_pool, w_out, w_gate_up, w_down, g_final)


@jax.jit
def _forward(x, c, ctx, c_ctx, w_ada, b_ada, g_norm1, w_in, lam_q1, lam_k1, lam_q2, lam_k2,
             g_subln, w_conv, w_pool, s_pool, w_out, g_norm2, w_gate_up, w_down, g_final):
    batch, seq_len, _ = x.shape
    ctx_len = ctx.shape[1]
    ctx_row = batch
    assert batch < MOD_ROWS
    tiles = _choose_tiles(seq_len, ctx_len)

    cc = jnp.concatenate([c, c_ctx[None, :], jnp.zeros((MOD_ROWS - batch - 1, D_MODEL), F32)], axis=0)
    mod = _ada_rows(cc, w_ada, b_ada).reshape(DEPTH, MOD_ROWS, N_MOD, 1, D_MODEL)

    w_in_b = w_in.astype(BF16)
    w_out_b = w_out.astype(BF16)
    w_gu_b = w_gate_up.astype(BF16)
    w_down_b = w_down.astype(BF16)
    eye = jnp.eye(POOL_GROUPS, dtype=F32)
    w_pool_bd = (w_pool[:, :, :, None, :] * eye[None, :, None, :, None]).reshape(
        DEPTH, POOL_WIDTH, POOL_WIDTH).astype(BF16)
    lam_rows = jnp.stack([lam_q1, lam_k1, lam_q2, lam_k2], axis=1)
    g1 = g_norm1[:, None, :]
    g2 = g_norm2[:, None, :]
    gs = g_subln[:, None, :]
    sp = s_pool[:, None, :]
    gf = g_final[None, :]
    tables = _rope_tables(seq_len)

    x_lat, x_ctx = x, ctx
    for l in range(DEPTH):
        last = l == DEPTH - 1
        lam_init = 0.8 - 0.6 * math.exp(-0.3 * l)
        ctx_proj = _inproj(x_ctx, mod, ctx_row, g1, w_in_b, None, layer=l, tm=tiles.tm_ctx, kv_only=last)
        q_l, k_l, vt_l, cp_l = _inproj(x_lat, mod, None, g1, w_in_b, tables, layer=l, tm=tiles.tm_lat)
        k_c, vt_c = ctx_proj[-3:-1] if not last else ctx_proj
        o_l = _attention(lam_rows, gs, q_l, [k_c, k_l], [vt_c, vt_l], layer=l, lam_init=lam_init,
                         tq=tiles.tq_lat, heads=1)
        mix_args = (g2, w_conv, w_pool_bd, sp, w_out_b, w_gu_b, w_down_b, gf)
        x_lat = _mixffn(x_lat, o_l, cp_l, mod, None, *mix_args, layer=l, tm=tiles.tm_lat, final=last)
        if not last:
            q_c, _, _, cp_c = ctx_proj
            o_c = _attention(lam_rows, gs, q_c, [k_c], [vt_c], layer=l, lam_init=lam_init,
                             tq=tiles.tq_ctx, heads=ATTN_HEADS)
            x_ctx = _mixffn(x_ctx, o_c, cp_c, mod, ctx_row, *mix_args, layer=l, tm=tiles.tm_ctx, final=False)
    return x_lat


def kernel(x, c, ctx, c_ctx, w_ada, b_ada, g_norm1, w_in, lam_q1, lam_k1, lam_q2, lam_k2, g_subln, w_conv,
           w_pool, s_pool, w_out, g_norm2, w_gate_up, w_down, g_final):
    return _forward(x, c, ctx, c_ctx, w_ada, b_ada, g_norm1, w_in, lam_q1, lam_k1, lam_q2, lam_k2, g_subln,
                    w_conv, w_pool, s_pool, w_out, g_norm2, w_gate_up, w_down, g_final)
```

```python
import functools
import math
from typing import NamedTuple

import jax
import jax.numpy as jnp
from jax import lax
from jax.experimental import pallas as pl
from jax.experimental.pallas import tpu as pltpu

D_MODEL = 1024
DEPTH = 4
GRID_W = 64
N_MOD = 6
ATTN_HEADS = 4
ATTN_QK_DIM = 64
ATTN_V_DIM = 2 * ATTN_QK_DIM
HEAD_COLS = 2 * ATTN_QK_DIM
Q_COLS = ATTN_HEADS * HEAD_COLS
V_COLS = ATTN_HEADS * ATTN_V_DIM
ATTN_SCALE = ATTN_QK_DIM ** -0.5
ROPE_THETA = 10000.0
ROPE_HALF = ATTN_QK_DIM // 4
CONV_WIDTH = D_MODEL // 4
POOL_WINDOWS = (2, 4, 8, 16)
POOL_GROUPS = len(POOL_WINDOWS)
POOL_WIDTH = D_MODEL - V_COLS - CONV_WIDTH
POOL_GROUP_DIM = POOL_WIDTH // POOL_GROUPS
CP_COLS = 3 * CONV_WIDTH + POOL_WIDTH
IN_COLS = 3 * Q_COLS + CP_COLS
FFN_HIDDEN = -(-(8 * D_MODEL) // (3 * 256)) * 256
EPS = 1e-6

V7X_LANES = 128
V7X_MXU_COLS = 256
V7X_BF16_SUBLANES = 16
V7X_VMEM_LIMIT_BYTES = 56 * 1024 * 1024

MOD_ROWS = 40
ONES_ROWS = V7X_BF16_SUBLANES
HALO = V7X_BF16_SUBLANES
FFN_CHUNK = V7X_MXU_COLS
N_FFN_CHUNKS = FFN_HIDDEN // FFN_CHUNK
ADA_COL_BLOCK = 1536
Q_PRESCALE = ATTN_SCALE * math.log2(math.e)

F32 = jnp.float32
BF16 = jnp.bfloat16


class Tiles(NamedTuple):
    tm_lat: int
    tm_ctx: int
    tq_lat: int
    tq_ctx: int


def _choose_tiles(seq_len, ctx_len):
    return Tiles(tm_lat=min(512, seq_len), tm_ctx=min(512, ctx_len),
                 tq_lat=min(V7X_MXU_COLS, seq_len), tq_ctx=min(V7X_MXU_COLS, ctx_len))


def _params(grid_rank=2):
    return pltpu.CompilerParams(dimension_semantics=("arbitrary",) * grid_rank,
                                vmem_limit_bytes=V7X_VMEM_LIMIT_BYTES)


def _dot(a, b):
    return jnp.dot(a, b, preferred_element_type=F32)


def _rms(x):
    return x * lax.rsqrt(jnp.mean(x * x, axis=-1, keepdims=True) + EPS)


def _layer_block(shape, layer, **kwargs):
    return pl.BlockSpec((None,) + tuple(shape), lambda bi, i: (layer,) + (0,) * len(shape), **kwargs)


def _mod_block(layer, which, mod_row):
    row = (lambda bi: bi) if mod_row is None else (lambda bi: mod_row)
    return pl.BlockSpec((None, None, None, 1, D_MODEL), lambda bi, i: (layer, row(bi), which, 0, 0))


def _ada_kernel(cc_ref, w_ref, b_ref, out_ref):
    cc = cc_ref[...]
    s = cc * jax.nn.sigmoid(cc)
    out_ref[0] = _dot(s.astype(BF16), w_ref[0].astype(BF16)) + b_ref[0]


def _ada_rows(cc, w_ada, b_ada):
    n_col = N_MOD * D_MODEL
    return pl.pallas_call(
        _ada_kernel,
        grid=(DEPTH, n_col // ADA_COL_BLOCK),
        in_specs=[
            pl.BlockSpec((MOD_ROWS, D_MODEL), lambda l, j: (0, 0)),
            pl.BlockSpec((1, D_MODEL, ADA_COL_BLOCK), lambda l, j: (l, 0, j)),
            pl.BlockSpec((1, 1, ADA_COL_BLOCK), lambda l, j: (l, 0, j)),
        ],
        out_specs=pl.BlockSpec((1, MOD_ROWS, ADA_COL_BLOCK), lambda l, j: (l, 0, j)),
        out_shape=jax.ShapeDtypeStruct((DEPTH, MOD_ROWS, n_col), F32),
        compiler_params=_params(),
        name="ada_rows",
    )(cc, w_ada, b_ada.reshape(DEPTH, 1, n_col))


def _rope_tables(length):
    pos = jnp.arange(length)
    row = (pos // GRID_W).astype(F32)
    col = (pos % GRID_W).astype(F32)
    inv = 1.0 / (ROPE_THETA ** (jnp.arange(ROPE_HALF, dtype=F32) / ROPE_HALF))
    ang_r = row[:, None] * inv[None, :]
    ang_c = col[:, None] * inv[None, :]
    ang = jnp.concatenate([ang_r, ang_r, ang_c, ang_c], axis=-1)
    cos = jnp.tile(jnp.cos(ang).astype(F32), (1, HEAD_COLS // ATTN_QK_DIM))
    sin = jnp.tile(jnp.sin(ang).astype(F32), (1, HEAD_COLS // ATTN_QK_DIM))
    first_half = (jnp.arange(HEAD_COLS) % (2 * ROPE_HALF)) < ROPE_HALF
    sin_from_upper = jnp.where(first_half, -sin, 0.0)
    sin_from_lower = jnp.where(first_half, 0.0, sin)
    return cos, sin_from_upper, sin_from_lower


def _inproj_kernel(*refs, rope, kv_only):
    x_ref, sh_ref, sc_ref, g_ref, w_ref = refs[:5]
    refs = refs[5:]
    if rope:
        cos_ref, sup_ref, slo_ref = refs[:3]
        refs = refs[3:]
    if kv_only:
        k_ref, vt_ref = refs
    else:
        q_ref, k_ref, vt_ref, cp_ref = refs

    h = (_rms(x_ref[0]) * g_ref[...]) * (1.0 + sc_ref[...]) + sh_ref[...]
    hb = h.astype(BF16)

    def rotate(t, scale):
        if not rope:
            return t if scale == 1.0 else t * scale
        out = []
        for j in range(Q_COLS // HEAD_COLS):
            tj = t[:, j * HEAD_COLS:(j + 1) * HEAD_COLS]
            rj = (tj * cos_ref[...]
                  + pltpu.roll(tj, HEAD_COLS - ROPE_HALF, 1) * sup_ref[...]
                  + pltpu.roll(tj, ROPE_HALF, 1) * slo_ref[...])
            out.append(rj if scale == 1.0 else rj * scale)
        return jnp.concatenate(out, axis=1)

    if not kv_only:
        q = _dot(hb, w_ref[:, 0:Q_COLS])
        q_ref[0] = rotate(q, Q_PRESCALE).astype(BF16)
    k = _dot(hb, w_ref[:, Q_COLS:2 * Q_COLS])
    k_ref[0] = rotate(k, 1.0).astype(BF16)
    v = _dot(hb, w_ref[:, 2 * Q_COLS:3 * Q_COLS])
    vt_ref[0] = v.T.astype(BF16)
    if not kv_only:
        cp_ref[0] = _dot(hb, w_ref[:, 3 * Q_COLS:IN_COLS]).astype(BF16)


def _inproj(x, mod, mod_row, g_norm1, w_in, tables, *, layer, tm, kv_only=False):
    b, length, _ = x.shape
    rope = tables is not None
    in_specs = [
        pl.BlockSpec((1, tm, D_MODEL), lambda bi, i: (bi, i, 0)),
        _mod_block(layer, 0, mod_row),
        _mod_block(layer, 1, mod_row),
        _layer_block((1, D_MODEL), layer),
        _layer_block((D_MODEL, IN_COLS), layer, pipeline_mode=pl.Buffered(1)),
    ]
    args = [x, mod, mod, g_norm1, w_in]
    if rope:
        in_specs += [pl.BlockSpec((tm, HEAD_COLS), lambda bi, i: (i, 0))] * 3
        args += list(tables)
    row_spec = lambda cols: pl.BlockSpec((1, tm, cols), lambda bi, i: (bi, i, 0))
    row_shape = lambda cols: jax.ShapeDtypeStruct((b, length, cols), BF16)
    vt_spec = pl.BlockSpec((1, V_COLS, tm), lambda bi, i: (bi, 0, i))
    vt_shape = jax.ShapeDtypeStruct((b, V_COLS, length), BF16)
    if kv_only:
        out_specs = [row_spec(Q_COLS), vt_spec]
        out_shape = [row_shape(Q_COLS), vt_shape]
    else:
        out_specs = [row_spec(Q_COLS), row_spec(Q_COLS), vt_spec, row_spec(CP_COLS)]
        out_shape = [row_shape(Q_COLS), row_shape(Q_COLS), vt_shape, row_shape(CP_COLS)]
    return pl.pallas_call(
        functools.partial(_inproj_kernel, rope=rope, kv_only=kv_only),
        grid=(b, length // tm),
        in_specs=in_specs,
        out_specs=out_specs,
        out_shape=out_shape,
        compiler_params=_params(),
        name="inproj_lat" if rope else "inproj_ctx",
    )(*args)


def _attn_kernel(*refs, key_lens, lam_init, tq, heads):
    n_src = len(key_lens)
    lam_ref, g_ref, q_ref = refs[:3]
    k_refs = refs[3:3 + n_src]
    vt_refs = refs[3 + n_src:3 + 2 * n_src]
    o_ref = refs[3 + 2 * n_src]
    k_all, vt_all = refs[4 + 2 * n_src:6 + 2 * n_src]
    scratch = refs[6 + 2 * n_src:]
    s_ref = (scratch[0:2], scratch[2:4])
    p_ref = (scratch[4:6], scratch[6:8])

    lq = lam_ref[...]
    lam = (jnp.exp(jnp.sum(lq[0:1] * lq[1:2], axis=1, keepdims=True))
           - jnp.exp(jnp.sum(lq[2:3] * lq[3:4], axis=1, keepdims=True)) + lam_init)
    lane = lax.broadcasted_iota(jnp.int32, (tq, HEAD_COLS), 1)
    n_q = q_ref.shape[1] // tq
    lk_total = sum(key_lens)
    vt_all[ATTN_V_DIM:, :] = jnp.ones((ONES_ROWS, lk_total), BF16)

    def one_head(head):
        cols = slice(head * HEAD_COLS, (head + 1) * HEAD_COLS)

        off = 0
        for k_ref, vt_ref, lk in zip(k_refs, vt_refs, key_lens):
            k_all[off:off + lk, :] = k_ref[0, :, cols]
            vt_all[0:ATTN_V_DIM, off:off + lk] = vt_ref[0, cols, :]
            off += lk

        def scores(i, slot):
            q = q_ref[0, pl.ds(pl.multiple_of(i * tq, tq), tq), cols]
            zero = jnp.zeros_like(q)
            maxima = []
            for c, keep in enumerate((lane < ATTN_QK_DIM, lane >= ATTN_QK_DIM)):
                s = lax.dot_general(k_all[...], jnp.where(keep, q, zero), (((1,), (1,)), ((), ())),
                                    preferred_element_type=F32)
                s_ref[slot][c][...] = s
                maxima.append(jnp.max(s, axis=0, keepdims=True))
            return tuple(maxima)

        def softmax(slot, maxima):
            for c in range(2):
                p_ref[slot][c][...] = jnp.exp2(s_ref[slot][c][...] - maxima[c]).astype(BF16)

        def values(i, slot):
            outs = []
            for c in range(2):
                av = _dot(vt_all[...], p_ref[slot][c][...])
                outs.append(av[0:ATTN_V_DIM] / av[ATTN_V_DIM:ATTN_V_DIM + 1])
            o_t = outs[0] - lam * outs[1]
            y = o_t * lax.rsqrt(jnp.mean(o_t * o_t, axis=0, keepdims=True) + EPS)
            o = (y.T * g_ref[...]) * (1.0 - lam_init)
            o_ref[0, pl.ds(pl.multiple_of(i * tq, tq), tq), cols] = o.astype(BF16)

        if n_q == 1:
            softmax(0, scores(0, 0))
            values(0, 0)
        else:
            assert n_q % 2 == 0
            max_even = scores(0, 0)
            max_odd = scores(1, 1)
            softmax(0, max_even)

            def pair(j, max_odd):
                max_even = scores(2 * j + 2, 0)
                softmax(1, max_odd)
                values(2 * j, 0)
                max_odd = scores(2 * j + 3, 1)
                softmax(0, max_even)
                values(2 * j + 1, 1)
                return max_odd

            for j in range(n_q // 2 - 1):
                max_odd = pair(j, max_odd)
            softmax(1, max_odd)
            values(n_q - 2, 0)
            values(n_q - 1, 1)

    for head in range(heads):
        one_head(head)


def _attention(lam_rows, g_subln, q, ks, vts, *, layer, lam_init, tq, heads):
    b, lq, _ = q.shape
    key_lens = tuple(k.shape[1] for k in ks)
    lk_total = sum(key_lens)
    cols = heads * HEAD_COLS
    head_rows = lambda length: pl.BlockSpec((1, length, cols), lambda bi, h: (bi, 0, h))
    in_specs = [
        _layer_block((4, ATTN_QK_DIM), layer),
        _layer_block((1, ATTN_V_DIM), layer),
        head_rows(lq),
    ]
    in_specs += [head_rows(lk) for lk in key_lens]
    in_specs += [pl.BlockSpec((1, cols, lk), lambda bi, h: (bi, h, 0)) for lk in key_lens]
    return pl.pallas_call(
        functools.partial(_attn_kernel, key_lens=key_lens, lam_init=lam_init, tq=tq, heads=heads),
        grid=(b, ATTN_HEADS // heads),
        in_specs=in_specs,
        out_specs=head_rows(lq),
        out_shape=jax.ShapeDtypeStruct((b, lq, V_COLS), BF16),
        scratch_shapes=[pltpu.VMEM((lk_total, HEAD_COLS), BF16), pltpu.VMEM((ATTN_V_DIM + ONES_ROWS, lk_total), BF16),
                        *[pltpu.VMEM((lk_total, tq), F32)] * 4, *[pltpu.VMEM((lk_total, tq), BF16)] * 4],
        compiler_params=_params(),
        name="diff_attn_lat" if len(ks) == 2 else "diff_attn_ctx",
    )(lam_rows, g_subln, q, *ks, *vts)


def _mixffn_kernel(x_ref, o_ref, cp_prev_ref, cp_ref, cp_next_ref, gt1_ref, sh2_ref, sc2_ref, gt2_ref,
                   g2_ref, wconv_ref, wpool_ref, spool_ref, wout_ref, wgu_ref, wd_ref, gfin_ref,
                   out_ref, acc_ref, *, length, tm, final):
    i = pl.program_id(1)
    n_tiles = length // tm
    rows = tm + 2 * HALO

    prev = jnp.where(i > 0, cp_prev_ref[0].astype(F32), 0.0)
    nxt = jnp.where(i < n_tiles - 1, cp_next_ref[0].astype(F32), 0.0)
    cur = cp_ref[0].astype(F32)
    ext = jnp.concatenate([prev, cur, nxt], axis=0)
    core = lambda a: a[HALO:HALO + tm]

    def shifted(a, d):
        return pltpu.roll(a, d % rows, 0)

    u = ext[:, CONV_WIDTH:2 * CONV_WIDTH] * ext[:, 2 * CONV_WIDTH:3 * CONV_WIDTH]
    wc = wconv_ref[...]
    y = shifted(u, 1) * wc[0:1] + u * wc[1:2] + shifted(u, -1) * wc[2:3]
    conv = cur[:, 0:CONV_WIDTH] * core(y)

    t = i * tm + lax.broadcasted_iota(jnp.int32, (tm, V7X_LANES), 0)
    lane = lax.broadcasted_iota(jnp.int32, (tm, V7X_LANES), 1)
    pooled = []
    for tile in range(POOL_WIDTH // V7X_LANES):
        p = ext[:, 3 * CONV_WIDTH + tile * V7X_LANES:3 * CONV_WIDTH + (tile + 1) * V7X_LANES]
        sums = {2: p + shifted(p, 1)}
        for w in (4, 8, 16):
            sums[w] = shifted(sums[w // 2], w // 4) + shifted(sums[w // 2], -(w // 4))
        w_lo, w_hi = POOL_WINDOWS[2 * tile], POOL_WINDOWS[2 * tile + 1]
        lower_group = lane < POOL_GROUP_DIM
        half = jnp.where(lower_group, w_lo // 2, w_hi // 2)
        cnt = jnp.minimum(t + half - 1, length - 1) - jnp.maximum(t - half, 0) + 1
        win_sum = jnp.where(lower_group, core(sums[w_lo]), core(sums[w_hi]))
        pooled.append(win_sum / cnt.astype(F32) - core(p))
    pooled = jnp.concatenate(pooled, axis=1).astype(BF16)
    pool_y = _dot(pooled, wpool_ref[...]) * spool_ref[...]

    wo = wout_ref
    mix = (_dot(o_ref[0], wo[0:V_COLS, :])
           + _dot(conv.astype(BF16), wo[V_COLS:V_COLS + CONV_WIDTH, :])
           + _dot(pool_y.astype(BF16), wo[V_COLS + CONV_WIDTH:D_MODEL, :]))
    x1 = x_ref[0] + gt1_ref[...] * mix
    out_ref[0] = x1
    h2 = ((_rms(x1) * g2_ref[...]) * (1.0 + sc2_ref[...]) + sh2_ref[...]).astype(BF16)

    for c in range(N_FFN_CHUNKS):
        gate = _dot(h2, wgu_ref[:, c * FFN_CHUNK:(c + 1) * FFN_CHUNK])
        up = _dot(h2, wgu_ref[:, FFN_HIDDEN + c * FFN_CHUNK:FFN_HIDDEN + (c + 1) * FFN_CHUNK])
        act = (gate * jax.nn.sigmoid(gate) * up).astype(BF16)
        part = _dot(act, wd_ref[c * FFN_CHUNK:(c + 1) * FFN_CHUNK, :])
        if c == 0:
            acc_ref[...] = part
        else:
            acc_ref[...] += part

    x2 = out_ref[0] + gt2_ref[...] * acc_ref[...]
    if final:
        x2 = _rms(x2) * gfin_ref[...]
    out_ref[0] = x2


def _mixffn(x, o, cp, mod, mod_row, g_norm2, w_conv, w_pool_bd, s_pool, w_out, w_gate_up, w_down, g_final,
            *, layer, tm, final):
    b, length, _ = x.shape
    n_tiles = length // tm
    halo_per_tile = tm // HALO
    n_halo_blocks = length // HALO
    weight = functools.partial(_layer_block, layer=layer, pipeline_mode=pl.Buffered(1))
    in_specs = [
        pl.BlockSpec((1, tm, D_MODEL), lambda bi, i: (bi, i, 0)),
        pl.BlockSpec((1, tm, V_COLS), lambda bi, i: (bi, i, 0)),
        pl.BlockSpec((1, HALO, CP_COLS), lambda bi, i: (bi, jnp.maximum(i * halo_per_tile - 1, 0), 0)),
        pl.BlockSpec((1, tm, CP_COLS), lambda bi, i: (bi, i, 0)),
        pl.BlockSpec((1, HALO, CP_COLS),
                     lambda bi, i: (bi, jnp.minimum((i + 1) * halo_per_tile, n_halo_blocks - 1), 0)),
    ] + [_mod_block(layer, which, mod_row) for which in (2, 3, 4, 5)] + [
        _layer_block((1, D_MODEL), layer),
        _layer_block((3, CONV_WIDTH), layer),
        weight((POOL_WIDTH, POOL_WIDTH)),
        _layer_block((1, POOL_WIDTH), layer),
        weight((D_MODEL, D_MODEL)),
        weight((D_MODEL, 2 * FFN_HIDDEN)),
        weight((FFN_HIDDEN, D_MODEL)),
        pl.BlockSpec((1, D_MODEL), lambda bi, i: (0, 0)),
    ]
    return pl.pallas_call(
        functools.partial(_mixffn_kernel, length=length, tm=tm, final=final),
        grid=(b, n_tiles),
        in_specs=in_specs,
        out_specs=pl.BlockSpec((1, tm, D_MODEL), lambda bi, i: (bi, i, 0)),
        out_shape=jax.ShapeDtypeStruct((b, length, D_MODEL), F32),
        scratch_shapes=[pltpu.VMEM((tm, D_MODEL), F32)],
        compiler_params=_params(),
        name="mix_ffn_lat" if mod_row is None else "mix_ffn_ctx",
    )(x, o, cp, cp, cp, mod, mod, mod, mod, g_norm2, w_conv, w_pool_bd, s_pool, w_out, w_gate_up, w_down, g_final)


@jax.jit
def _forward(x, c, ctx, c_ctx, w_ada, b_ada, g_norm1, w_in, lam_q1, lam_k1, lam_q2, lam_k2,
             g_subln, w_conv, w_pool, s_pool, w_out, g_norm2, w_gate_up, w_down, g_final):
    batch, seq_len, _ = x.shape
    ctx_len = ctx.shape[1]
    ctx_row = batch
    assert batch < MOD_ROWS
    tiles = _choose_tiles(seq_len, ctx_len)

    cc = jnp.concatenate([c, c_ctx[None, :], jnp.zeros((MOD_ROWS - batch - 1, D_MODEL), F32)], axis=0)
    mod = _ada_rows(cc, w_ada, b_ada).reshape(DEPTH, MOD_ROWS, N_MOD, 1, D_MODEL)

    w_in_b = w_in.astype(BF16)
    w_out_b = w_out.astype(BF16)
    w_gu_b = w_gate_up.astype(BF16)
    w_down_b = w_down.astype(BF16)
    eye = jnp.eye(POOL_GROUPS, dtype=F32)
    w_pool_bd = (w_pool[:, :, :, None, :] * eye[None, :, None, :, None]).reshape(
        DEPTH, POOL_WIDTH, POOL_WIDTH).astype(BF16)
    lam_rows = jnp.stack([lam_q1, lam_k1, lam_q2, lam_k2], axis=1)
    g1 = g_norm1[:, None, :]
    g2 = g_norm2[:, None, :]
    gs = g_subln[:, None, :]
    sp = s_pool[:, None, :]
    gf = g_final[None, :]
    tables = _rope_tables(seq_len)

    x_lat, x_ctx = x, ctx
    for l in range(DEPTH):
        last = l == DEPTH - 1
        lam_init = 0.8 - 0.6 * math.exp(-0.3 * l)
        ctx_proj = _inproj(x_ctx, mod, ctx_row, g1, w_in_b, None, layer=l, tm=tiles.tm_ctx, kv_only=last)
        q_l, k_l, vt_l, cp_l = _inproj(x_lat, mod, None, g1, w_in_b, tables, layer=l, tm=tiles.tm_lat)
        k_c, vt_c = ctx_proj[-3:-1] if not last else ctx_proj
        o_l = _attention(lam_rows, gs, q_l, [k_c, k_l], [vt_c, vt_l], layer=l, lam_init=lam_init,
                         tq=tiles.tq_lat, heads=1)
        mix_args = (g2, w_conv, w_pool_bd, sp, w_out_b, w_gu_b, w_down_b, gf)
        x_lat = _mixffn(x_lat, o_l, cp_l, mod, None, *mix_args, layer=l, tm=tiles.tm_lat, final=last)
        if not last:
            q_c, _, _, cp_c = ctx_proj
            o_c = _attention(lam_rows, gs, q_c, [k_c], [vt_c], layer=l, lam_init=lam_init,
                             tq=tiles.tq_ctx, heads=ATTN_HEADS)
            x_ctx = _mixffn(x_ctx, o_c, cp_c, mod, ctx_row, *mix_args, layer=l, tm=tiles.tm_ctx, final=False)
    return x_lat


def kernel(x, c, ctx, c_ctx, w_ada, b_ada, g_norm1, w_in, lam_q1, lam_k1, lam_q2, lam_k2, g_subln, w_conv,
           w_pool, s_pool, w_out, g_norm2, w_gate_up, w_down, g_final):
    return _forward(x, c, ctx, c_ctx, w_ada, b_ada, g_norm1, w_in, lam_q1, lam_k1, lam_q2, lam_k2, g_subln,
                    w_conv, w_pool, s_pool, w_out, g_norm2, w_gate_up, w_down, g_final)
```

```python
import functools
import math
from typing import NamedTuple

import jax
import jax.numpy as jnp
from jax import lax
from jax.experimental import pallas as pl
from jax.experimental.pallas import tpu as pltpu

D_MODEL = 1024
DEPTH = 4
GRID_W = 64
N_MOD = 6
ATTN_HEADS = 4
ATTN_QK_DIM = 64
ATTN_V_DIM = 2 * ATTN_QK_DIM
HEAD_COLS = 2 * ATTN_QK_DIM
Q_COLS = ATTN_HEADS * HEAD_COLS
V_COLS = ATTN_HEADS * ATTN_V_DIM
ATTN_SCALE = ATTN_QK_DIM ** -0.5
ROPE_THETA = 10000.0
ROPE_HALF = ATTN_QK_DIM // 4
CONV_WIDTH = D_MODEL // 4
POOL_WINDOWS = (2, 4, 8, 16)
POOL_GROUPS = len(POOL_WINDOWS)
POOL_WIDTH = D_MODEL - V_COLS - CONV_WIDTH
POOL_GROUP_DIM = POOL_WIDTH // POOL_GROUPS
CP_COLS = 3 * CONV_WIDTH + POOL_WIDTH
IN_COLS = 3 * Q_COLS + CP_COLS
FFN_HIDDEN = -(-(8 * D_MODEL) // (3 * 256)) * 256
EPS = 1e-6

V7X_LANES = 128
V7X_MXU_COLS = 256
V7X_BF16_SUBLANES = 16
V7X_VMEM_LIMIT_BYTES = 56 * 1024 * 1024

MOD_ROWS = 40
ONES_ROWS = V7X_BF16_SUBLANES
KV_SETS = 3
HALO = V7X_BF16_SUBLANES
FFN_CHUNK = V7X_MXU_COLS
N_FFN_CHUNKS = FFN_HIDDEN // FFN_CHUNK
ADA_COL_BLOCK = 1536
Q_PRESCALE = ATTN_SCALE * math.log2(math.e)

F32 = jnp.float32
BF16 = jnp.bfloat16


class Tiles(NamedTuple):
    tm_in: int
    tm_lat: int
    tm_ctx: int
    tq_lat: int
    tq_ctx: int
    heads_lat: int
    sub_lat: int


def _choose_tiles(seq_len, ctx_len):
    return Tiles(tm_in=min(1024, seq_len), tm_lat=min(512, seq_len), tm_ctx=min(512, ctx_len),
                 tq_lat=min(V7X_MXU_COLS, seq_len), tq_ctx=min(V7X_MXU_COLS, ctx_len), heads_lat=1,
                 sub_lat=1)


def _params(grid_rank=2):
    return pltpu.CompilerParams(dimension_semantics=("arbitrary",) * grid_rank,
                                vmem_limit_bytes=V7X_VMEM_LIMIT_BYTES)


def _dot(a, b):
    return jnp.dot(a, b, preferred_element_type=F32)


def _rms(x):
    return x * lax.rsqrt(jnp.mean(x * x, axis=-1, keepdims=True) + EPS)


def _layer_block(shape, layer, **kwargs):
    return pl.BlockSpec((None,) + tuple(shape), lambda bi, i: (layer,) + (0,) * len(shape), **kwargs)


def _mod_block(layer, which, mod_row):
    row = (lambda bi: bi) if mod_row is None else (lambda bi: mod_row)
    return pl.BlockSpec((None, None, None, 1, D_MODEL), lambda bi, i: (layer, row(bi), which, 0, 0))


def _ada_kernel(cc_ref, w_ref, b_ref, out_ref):
    cc = cc_ref[...]
    s = cc * jax.nn.sigmoid(cc)
    out_ref[0] = _dot(s.astype(BF16), w_ref[0].astype(BF16)) + b_ref[0]


def _ada_rows(cc, w_ada, b_ada):
    n_col = N_MOD * D_MODEL
    return pl.pallas_call(
        _ada_kernel,
        grid=(DEPTH, n_col // ADA_COL_BLOCK),
        in_specs=[
            pl.BlockSpec((MOD_ROWS, D_MODEL), lambda l, j: (0, 0)),
            pl.BlockSpec((1, D_MODEL, ADA_COL_BLOCK), lambda l, j: (l, 0, j)),
            pl.BlockSpec((1, 1, ADA_COL_BLOCK), lambda l, j: (l, 0, j)),
        ],
        out_specs=pl.BlockSpec((1, MOD_ROWS, ADA_COL_BLOCK), lambda l, j: (l, 0, j)),
        out_shape=jax.ShapeDtypeStruct((DEPTH, MOD_ROWS, n_col), F32),
        compiler_params=_params(),
        name="ada_rows",
    )(cc, w_ada, b_ada.reshape(DEPTH, 1, n_col))


def _rope_tables(length):
    pos = jnp.arange(length)
    row = (pos // GRID_W).astype(F32)
    col = (pos % GRID_W).astype(F32)
    inv = 1.0 / (ROPE_THETA ** (jnp.arange(ROPE_HALF, dtype=F32) / ROPE_HALF))
    ang_r = row[:, None] * inv[None, :]
    ang_c = col[:, None] * inv[None, :]
    ang = jnp.concatenate([ang_r, ang_r, ang_c, ang_c], axis=-1)
    cos = jnp.tile(jnp.cos(ang).astype(F32), (1, HEAD_COLS // ATTN_QK_DIM))
    sin = jnp.tile(jnp.sin(ang).astype(F32), (1, HEAD_COLS // ATTN_QK_DIM))
    first_half = (jnp.arange(HEAD_COLS) % (2 * ROPE_HALF)) < ROPE_HALF
    sin_from_upper = jnp.where(first_half, -sin, 0.0)
    sin_from_lower = jnp.where(first_half, 0.0, sin)
    return cos, sin_from_upper, sin_from_lower


def _inproj_kernel(*refs, rope, kv_only):
    x_ref, sh_ref, sc_ref, g_ref, w_ref = refs[:5]
    refs = refs[5:]
    if rope:
        cos_ref, sup_ref, slo_ref = refs[:3]
        refs = refs[3:]
    if kv_only:
        k_ref, vt_ref = refs
    else:
        q_ref, k_ref, vt_ref, cp_ref = refs

    h = _rms(x_ref[0]) * (g_ref[...] * (1.0 + sc_ref[...])) + sh_ref[...]
    hb = h.astype(BF16)

    def rotate(t, scale):
        if not rope:
            return t if scale == 1.0 else t * scale
        out = []
        for j in range(Q_COLS // HEAD_COLS):
            tj = t[:, j * HEAD_COLS:(j + 1) * HEAD_COLS]
            rj = (tj * cos_ref[...]
                  + pltpu.roll(tj, HEAD_COLS - ROPE_HALF, 1) * sup_ref[...]
                  + pltpu.roll(tj, ROPE_HALF, 1) * slo_ref[...])
            out.append(rj if scale == 1.0 else rj * scale)
        return jnp.concatenate(out, axis=1)

    if not kv_only:
        q = _dot(hb, w_ref[:, 0:Q_COLS])
        q_ref[0] = rotate(q, Q_PRESCALE).astype(BF16)
    k = _dot(hb, w_ref[:, Q_COLS:2 * Q_COLS])
    k_ref[0] = rotate(k, 1.0).astype(BF16)
    v = _dot(hb, w_ref[:, 2 * Q_COLS:3 * Q_COLS])
    vt_ref[0] = v.T.astype(BF16)
    if not kv_only:
        cp_ref[0] = _dot(hb, w_ref[:, 3 * Q_COLS:IN_COLS]).astype(BF16)


def _inproj(x, mod, mod_row, g_norm1, w_in, tables, *, layer, tm, kv_only=False):
    b, length, _ = x.shape
    rope = tables is not None
    in_specs = [
        pl.BlockSpec((1, tm, D_MODEL), lambda bi, i: (bi, i, 0)),
        _mod_block(layer, 0, mod_row),
        _mod_block(layer, 1, mod_row),
        _layer_block((1, D_MODEL), layer),
        _layer_block((D_MODEL, IN_COLS), layer, pipeline_mode=pl.Buffered(1)),
    ]
    args = [x, mod, mod, g_norm1, w_in]
    if rope:
        in_specs += [pl.BlockSpec((tm, HEAD_COLS), lambda bi, i: (i, 0))] * 3
        args += list(tables)
    row_spec = lambda cols: pl.BlockSpec((1, tm, cols), lambda bi, i: (bi, i, 0))
    row_shape = lambda cols: jax.ShapeDtypeStruct((b, length, cols), BF16)
    vt_spec = pl.BlockSpec((1, V_COLS, tm), lambda bi, i: (bi, 0, i))
    vt_shape = jax.ShapeDtypeStruct((b, V_COLS, length), BF16)
    if kv_only:
        out_specs = [row_spec(Q_COLS), vt_spec]
        out_shape = [row_shape(Q_COLS), vt_shape]
    else:
        out_specs = [row_spec(Q_COLS), row_spec(Q_COLS), vt_spec, row_spec(CP_COLS)]
        out_shape = [row_shape(Q_COLS), row_shape(Q_COLS), vt_shape, row_shape(CP_COLS)]
    return pl.pallas_call(
        functools.partial(_inproj_kernel, rope=rope, kv_only=kv_only),
        grid=(b, length // tm),
        in_specs=in_specs,
        out_specs=out_specs,
        out_shape=out_shape,
        compiler_params=_params(),
        name="inproj_lat" if rope else "inproj_ctx",
    )(*args)


def _attn_kernel(*refs, key_lens, lam_init, tq, heads):
    n_src = len(key_lens)
    lam_ref, g_ref, q_ref = refs[:3]
    k_refs = refs[3:3 + n_src]
    vt_refs = refs[3 + n_src:3 + 2 * n_src]
    o_ref = refs[3 + 2 * n_src]
    scratch = refs[4 + 2 * n_src:]
    k_all, vt_all = scratch[0:KV_SETS], scratch[KV_SETS:2 * KV_SETS]
    scratch = scratch[2 * KV_SETS:]
    s_ref = (scratch[0:2], scratch[2:4])
    p_ref = (scratch[4:6], scratch[6:8])

    lq = lam_ref[...]
    lam = (jnp.exp(jnp.sum(lq[0:1] * lq[1:2], axis=1, keepdims=True))
           - jnp.exp(jnp.sum(lq[2:3] * lq[3:4], axis=1, keepdims=True)) + lam_init)
    lane = lax.broadcasted_iota(jnp.int32, (tq, HEAD_COLS), 1)
    n_q = q_ref.shape[1] // tq
    lk_total = sum(key_lens)
    tiles = [(head, i) for head in range(heads) for i in range(n_q)]
    head_cols = lambda head: slice(head * HEAD_COLS, (head + 1) * HEAD_COLS)
    q_rows = lambda i: slice(i * tq, (i + 1) * tq)

    def load_head(head):
        off = 0
        for k_ref, vt_ref, lk in zip(k_refs, vt_refs, key_lens):
            k_all[head % KV_SETS][off:off + lk, :] = k_ref[0, :, head_cols(head)]
            vt_all[head % KV_SETS][0:ATTN_V_DIM, off:off + lk] = vt_ref[0, head_cols(head), :]
            off += lk
        vt_all[head % KV_SETS][ATTN_V_DIM:, :] = jnp.ones((ONES_ROWS, lk_total), BF16)

    def scores(t):
        head, i = tiles[t]
        if i == 0:
            load_head(head)
        q = q_ref[0, q_rows(i), head_cols(head)]
        zero = jnp.zeros_like(q)
        maxima = []
        for c, keep in enumerate((lane < ATTN_QK_DIM, lane >= ATTN_QK_DIM)):
            s = lax.dot_general(k_all[head % KV_SETS][...], jnp.where(keep, q, zero),
                                (((1,), (1,)), ((), ())), preferred_element_type=F32)
            s_ref[t % 2][c][...] = s
            maxima.append(jnp.max(s, axis=0, keepdims=True))
        return tuple(maxima)

    def softmax(t, maxima):
        for c in range(2):
            p_ref[t % 2][c][...] = jnp.exp2(s_ref[t % 2][c][...] - maxima[c]).astype(BF16)

    def values(t):
        head, i = tiles[t]
        outs = []
        for c in range(2):
            av = _dot(vt_all[head % KV_SETS][...], p_ref[t % 2][c][...])
            outs.append(av[0:ATTN_V_DIM] / av[ATTN_V_DIM:ATTN_V_DIM + 1])
        o_t = outs[0] - lam * outs[1]
        y = o_t * lax.rsqrt(jnp.mean(o_t * o_t, axis=0, keepdims=True) + EPS)
        o = (y.T * g_ref[...]) * (1.0 - lam_init)
        o_ref[0, q_rows(i), head_cols(head)] = o.astype(BF16)

    n = len(tiles)
    maxima = {0: scores(0)}
    if n > 1:
        maxima[1] = scores(1)
    softmax(0, maxima[0])
    for t in range(n):
        if t + 2 < n:
            maxima[t + 2] = scores(t + 2)
        if t + 1 < n:
            softmax(t + 1, maxima[t + 1])
        values(t)


def _attention(lam_rows, g_subln, q, ks, vts, *, layer, lam_init, tq, heads):
    b, lq, _ = q.shape
    key_lens = tuple(k.shape[1] for k in ks)
    lk_total = sum(key_lens)
    cols = heads * HEAD_COLS
    head_rows = lambda length: pl.BlockSpec((1, length, cols), lambda bi, h: (bi, 0, h))
    in_specs = [
        _layer_block((4, ATTN_QK_DIM), layer),
        _layer_block((1, ATTN_V_DIM), layer),
        head_rows(lq),
    ]
    in_specs += [head_rows(lk) for lk in key_lens]
    in_specs += [pl.BlockSpec((1, cols, lk), lambda bi, h: (bi, h, 0)) for lk in key_lens]
    return pl.pallas_call(
        functools.partial(_attn_kernel, key_lens=key_lens, lam_init=lam_init, tq=tq, heads=heads),
        grid=(b, ATTN_HEADS // heads),
        in_specs=in_specs,
        out_specs=head_rows(lq),
        out_shape=jax.ShapeDtypeStruct((b, lq, V_COLS), BF16),
        scratch_shapes=[*[pltpu.VMEM((lk_total, HEAD_COLS), BF16)] * KV_SETS,
                        *[pltpu.VMEM((ATTN_V_DIM + ONES_ROWS, lk_total), BF16)] * KV_SETS,
                        *[pltpu.VMEM((lk_total, tq), F32)] * 4, *[pltpu.VMEM((lk_total, tq), BF16)] * 4],
        compiler_params=_params(),
        name="diff_attn_lat" if len(ks) == 2 else "diff_attn_ctx",
    )(lam_rows, g_subln, q, *ks, *vts)


def _mixffn_kernel(x_ref, o_ref, cp_prev_ref, cp_ref, cp_next_ref, gt1_ref, sh2_ref, sc2_ref, gt2_ref,
                   g2_ref, wconv_ref, wpool_ref, spool_ref, wout_ref, wgu_ref, wd_ref, gfin_ref,
                   out_ref, *acc_refs, length, tm, final):
    i = pl.program_id(1)
    n_tiles = length // tm
    n_sub = len(acc_refs)
    ts = tm // n_sub
    rows = ts + 2 * HALO
    core = lambda a: a[HALO:HALO + ts]

    def shifted(a, d):
        return pltpu.roll(a, d % rows, 0)

    def sub_tile(k):
        r0 = k * ts
        tile_rows = slice(r0, r0 + ts)
        if k == 0:
            before = jnp.where(i > 0, cp_prev_ref[0].astype(F32), 0.0)
        else:
            before = cp_ref[0, r0 - HALO:r0, :].astype(F32)
        if k == n_sub - 1:
            after = jnp.where(i < n_tiles - 1, cp_next_ref[0].astype(F32), 0.0)
        else:
            after = cp_ref[0, r0 + ts:r0 + ts + HALO, :].astype(F32)
        cur = cp_ref[0, tile_rows, :].astype(F32)
        ext = jnp.concatenate([before, cur, after], axis=0)

        u = ext[:, CONV_WIDTH:2 * CONV_WIDTH] * ext[:, 2 * CONV_WIDTH:3 * CONV_WIDTH]
        wc = wconv_ref[...]
        y = shifted(u, 1) * wc[0:1] + u * wc[1:2] + shifted(u, -1) * wc[2:3]
        conv = cur[:, 0:CONV_WIDTH] * core(y)

        t = i * tm + r0 + lax.broadcasted_iota(jnp.int32, (ts, V7X_LANES), 0)
        lane = lax.broadcasted_iota(jnp.int32, (ts, V7X_LANES), 1)
        pooled = []
        for tile in range(POOL_WIDTH // V7X_LANES):
            p = ext[:, 3 * CONV_WIDTH + tile * V7X_LANES:3 * CONV_WIDTH + (tile + 1) * V7X_LANES]
            sums = {2: p + shifted(p, 1)}
            for w in (4, 8, 16):
                sums[w] = shifted(sums[w // 2], w // 4) + shifted(sums[w // 2], -(w // 4))
            w_lo, w_hi = POOL_WINDOWS[2 * tile], POOL_WINDOWS[2 * tile + 1]
            lower_group = lane < POOL_GROUP_DIM
            half = jnp.where(lower_group, w_lo // 2, w_hi // 2)
            cnt = jnp.minimum(t + half - 1, length - 1) - jnp.maximum(t - half, 0) + 1
            win_sum = jnp.where(lower_group, core(sums[w_lo]), core(sums[w_hi]))
            pooled.append(win_sum / cnt.astype(F32) - core(p))
        pooled = jnp.concatenate(pooled, axis=1).astype(BF16)
        pool_y = _dot(pooled, wpool_ref[...]) * spool_ref[...]

        wo = wout_ref
        mix = (_dot(o_ref[0, tile_rows, :], wo[0:V_COLS, :])
               + _dot(conv.astype(BF16), wo[V_COLS:V_COLS + CONV_WIDTH, :])
               + _dot(pool_y.astype(BF16), wo[V_COLS + CONV_WIDTH:D_MODEL, :]))
        x1 = x_ref[0, tile_rows, :] + gt1_ref[...] * mix
        out_ref[0, tile_rows, :] = x1
        h2 = (_rms(x1) * (g2_ref[...] * (1.0 + sc2_ref[...])) + sh2_ref[...]).astype(BF16)

        acc_ref = acc_refs[k]
        for c in range(N_FFN_CHUNKS):
            gate = _dot(h2, wgu_ref[:, c * FFN_CHUNK:(c + 1) * FFN_CHUNK])
            up = _dot(h2, wgu_ref[:, FFN_HIDDEN + c * FFN_CHUNK:FFN_HIDDEN + (c + 1) * FFN_CHUNK])
            act = (gate * jax.nn.sigmoid(gate) * up).astype(BF16)
            part = _dot(act, wd_ref[c * FFN_CHUNK:(c + 1) * FFN_CHUNK, :])
            if c == 0:
                acc_ref[...] = part
            else:
                acc_ref[...] += part

        x2 = out_ref[0, tile_rows, :] + gt2_ref[...] * acc_ref[...]
        if final:
            x2 = _rms(x2) * gfin_ref[...]
        out_ref[0, tile_rows, :] = x2

    for k in range(n_sub):
        sub_tile(k)


def _mixffn(x, o, cp, mod, mod_row, g_norm2, w_conv, w_pool_bd, s_pool, w_out, w_gate_up, w_down, g_final,
            *, layer, tm, n_sub, final):
    b, length, _ = x.shape
    n_tiles = length // tm
    halo_per_tile = tm // HALO
    n_halo_blocks = length // HALO
    weight = functools.partial(_layer_block, layer=layer, pipeline_mode=pl.Buffered(1))
    in_specs = [
        pl.BlockSpec((1, tm, D_MODEL), lambda bi, i: (bi, i, 0)),
        pl.BlockSpec((1, tm, V_COLS), lambda bi, i: (bi, i, 0)),
        pl.BlockSpec((1, HALO, CP_COLS), lambda bi, i: (bi, jnp.maximum(i * halo_per_tile - 1, 0), 0)),
        pl.BlockSpec((1, tm, CP_COLS), lambda bi, i: (bi, i, 0)),
        pl.BlockSpec((1, HALO, CP_COLS),
                     lambda bi, i: (bi, jnp.minimum((i + 1) * halo_per_tile, n_halo_blocks - 1), 0)),
    ] + [_mod_block(layer, which, mod_row) for which in (2, 3, 4, 5)] + [
        _layer_block((1, D_MODEL), layer),
        _layer_block((3, CONV_WIDTH), layer),
        weight((POOL_WIDTH, POOL_WIDTH)),
        _layer_block((1, POOL_WIDTH), layer),
        weight((D_MODEL, D_MODEL)),
        weight((D_MODEL, 2 * FFN_HIDDEN)),
        weight((FFN_HIDDEN, D_MODEL)),
        pl.BlockSpec((1, D_MODEL), lambda bi, i: (0, 0)),
    ]
    return pl.pallas_call(
        functools.partial(_mixffn_kernel, length=length, tm=tm, final=final),
        grid=(b, n_tiles),
        in_specs=in_specs,
        out_specs=pl.BlockSpec((1, tm, D_MODEL), lambda bi, i: (bi, i, 0)),
        out_shape=jax.ShapeDtypeStruct((b, length, D_MODEL), F32),
        scratch_shapes=[pltpu.VMEM((tm // n_sub, D_MODEL), F32)] * n_sub,
        compiler_params=_params(),
        name="mix_ffn_lat" if mod_row is None else "mix_ffn_ctx",
    )(x, o, cp, cp, cp, mod, mod, mod, mod, g_norm2, w_conv, w_pool_bd, s_pool, w_out, w_gate_up, w_down, g_final)


@jax.jit
def _forward(x, c, ctx, c_ctx, w_ada, b_ada, g_norm1, w_in, lam_q1, lam_k1, lam_q2, lam_k2,
             g_subln, w_conv, w_pool, s_pool, w_out, g_norm2, w_gate_up, w_down, g_final):
    batch, seq_len, _ = x.shape
    ctx_len = ctx.shape[1]
    ctx_row = batch
    assert batch < MOD_ROWS
    tiles = _choose_tiles(seq_len, ctx_len)

    cc = jnp.concatenate([c, c_ctx[None, :], jnp.zeros((MOD_ROWS - batch - 1, D_MODEL), F32)], axis=0)
    mod = _ada_rows(cc, w_ada, b_ada).reshape(DEPTH, MOD_ROWS, N_MOD, 1, D_MODEL)

    w_in_b = w_in.astype(BF16)
    w_out_b = w_out.astype(BF16)
    w_gu_b = w_gate_up.astype(BF16)
    w_down_b = w_down.astype(BF16)
    eye = jnp.eye(POOL_GROUPS, dtype=F32)
    w_pool_bd = (w_pool[:, :, :, None, :] * eye[None, :, None, :, None]).reshape(
        DEPTH, POOL_WIDTH, POOL_WIDTH).astype(BF16)
    lam_rows = jnp.stack([lam_q1, lam_k1, lam_q2, lam_k2], axis=1)
    g1 = g_norm1[:, None, :]
    g2 = g_norm2[:, None, :]
    gs = g_subln[:, None, :]
    sp = s_pool[:, None, :]
    gf = g_final[None, :]
    tables = _rope_tables(seq_len)

    x_lat, x_ctx = x, ctx
    for l in range(DEPTH):
        last = l == DEPTH - 1
        lam_init = 0.8 - 0.6 * math.exp(-0.3 * l)
        ctx_proj = _inproj(x_ctx, mod, ctx_row, g1, w_in_b, None, layer=l, tm=tiles.tm_ctx, kv_only=last)
        q_l, k_l, vt_l, cp_l = _inproj(x_lat, mod, None, g1, w_in_b, tables, layer=l, tm=tiles.tm_in)
        k_c, vt_c = ctx_proj[-3:-1] if not last else ctx_proj
        o_l = _attention(lam_rows, gs, q_l, [k_c, k_l], [vt_c, vt_l], layer=l, lam_init=lam_init,
                         tq=tiles.tq_lat, heads=tiles.heads_lat)
        mix_args = (g2, w_conv, w_pool_bd, sp, w_out_b, w_gu_b, w_down_b, gf)
        x_lat = _mixffn(x_lat, o_l, cp_l, mod, None, *mix_args, layer=l, tm=tiles.tm_lat, n_sub=tiles.sub_lat,
                        final=last)
        if not last:
            q_c, _, _, cp_c = ctx_proj
            o_c = _attention(lam_rows, gs, q_c, [k_c], [vt_c], layer=l, lam_init=lam_init,
                             tq=tiles.tq_ctx, heads=ATTN_HEADS)
            x_ctx = _mixffn(x_ctx, o_c, cp_c, mod, ctx_row, *mix_args, layer=l, tm=tiles.tm_ctx, n_sub=1,
                            final=False)
    return x_lat


def kernel(x, c, ctx, c_ctx, w_ada, b_ada, g_norm1, w_in, lam_q1, lam_k1, lam_q2, lam_k2, g_subln, w_conv,
           w_pool, s_pool, w_out, g_norm2, w_gate_up, w_down, g_final):
    return _forward(x, c, ctx, c_ctx, w_ada, b_ada, g_norm1, w_in, lam_q1, lam_k1, lam_q2, lam_k2, g_subln,
                    w_conv, w_pool, s_pool, w_out, g_norm2, w_gate_up, w_down, g_final)
```

```python
import functools
import math
from typing import NamedTuple

import jax
import jax.numpy as jnp
from jax import lax
from jax.experimental import pallas as pl
from jax.experimental.pallas import tpu as pltpu

D_MODEL = 1024
DEPTH = 4
GRID_W = 64
N_MOD = 6
ATTN_HEADS = 4
ATTN_QK_DIM = 64
ATTN_V_DIM = 2 * ATTN_QK_DIM
HEAD_COLS = 2 * ATTN_QK_DIM
Q_COLS = ATTN_HEADS * HEAD_COLS
V_COLS = ATTN_HEADS * ATTN_V_DIM
ATTN_SCALE = ATTN_QK_DIM ** -0.5
ROPE_THETA = 10000.0
ROPE_HALF = ATTN_QK_DIM // 4
CONV_WIDTH = D_MODEL // 4
POOL_WINDOWS = (2, 4, 8, 16)
POOL_GROUPS = len(POOL_WINDOWS)
POOL_WIDTH = D_MODEL - V_COLS - CONV_WIDTH
POOL_GROUP_DIM = POOL_WIDTH // POOL_GROUPS
CP_COLS = 3 * CONV_WIDTH + POOL_WIDTH
IN_COLS = 3 * Q_COLS + CP_COLS
FFN_HIDDEN = -(-(8 * D_MODEL) // (3 * 256)) * 256
EPS = 1e-6

V7X_LANES = 128
V7X_MXU_COLS = 256
V7X_BF16_SUBLANES = 16
V7X_VMEM_LIMIT_BYTES = 56 * 1024 * 1024

MOD_ROWS = 40
ONES_ROWS = V7X_BF16_SUBLANES
KV_SETS = 3
HALO = V7X_BF16_SUBLANES
FFN_CHUNK = V7X_MXU_COLS
N_FFN_CHUNKS = FFN_HIDDEN // FFN_CHUNK
ADA_COL_BLOCK = 1536
Q_PRESCALE = ATTN_SCALE * math.log2(math.e)

F32 = jnp.float32
BF16 = jnp.bfloat16


class Tiles(NamedTuple):
    tm_in: int
    tm_lat: int
    tm_ctx: int
    tq_lat: int
    tq_ctx: int
    heads_lat: int
    sub_lat: int


def _choose_tiles(seq_len, ctx_len):
    return Tiles(tm_in=min(1024, seq_len), tm_lat=min(512, seq_len), tm_ctx=min(512, ctx_len),
                 tq_lat=min(V7X_MXU_COLS, seq_len), tq_ctx=min(V7X_MXU_COLS, ctx_len), heads_lat=1,
                 sub_lat=1)


def _params(grid_rank=2):
    return pltpu.CompilerParams(dimension_semantics=("arbitrary",) * grid_rank,
                                vmem_limit_bytes=V7X_VMEM_LIMIT_BYTES)


def _dot(a, b):
    return jnp.dot(a, b, preferred_element_type=F32)


def _rms(x):
    return x * lax.rsqrt(jnp.mean(x * x, axis=-1, keepdims=True) + EPS)


def _layer_block(shape, layer, **kwargs):
    return pl.BlockSpec((None,) + tuple(shape), lambda bi, i: (layer,) + (0,) * len(shape), **kwargs)


def _mod_block(layer, which, mod_row):
    row = (lambda bi: bi) if mod_row is None else (lambda bi: mod_row)
    return pl.BlockSpec((None, None, None, 1, D_MODEL), lambda bi, i: (layer, row(bi), which, 0, 0))


def _ada_kernel(cc_ref, w_ref, b_ref, out_ref):
    cc = cc_ref[...]
    s = cc * jax.nn.sigmoid(cc)
    out_ref[0] = _dot(s.astype(BF16), w_ref[0].astype(BF16)) + b_ref[0]


def _ada_rows(cc, w_ada, b_ada):
    n_col = N_MOD * D_MODEL
    return pl.pallas_call(
        _ada_kernel,
        grid=(DEPTH, n_col // ADA_COL_BLOCK),
        in_specs=[
            pl.BlockSpec((MOD_ROWS, D_MODEL), lambda l, j: (0, 0)),
            pl.BlockSpec((1, D_MODEL, ADA_COL_BLOCK), lambda l, j: (l, 0, j)),
            pl.BlockSpec((1, 1, ADA_COL_BLOCK), lambda l, j: (l, 0, j)),
        ],
        out_specs=pl.BlockSpec((1, MOD_ROWS, ADA_COL_BLOCK), lambda l, j: (l, 0, j)),
        out_shape=jax.ShapeDtypeStruct((DEPTH, MOD_ROWS, n_col), F32),
        compiler_params=_params(),
        name="ada_rows",
    )(cc, w_ada, b_ada.reshape(DEPTH, 1, n_col))


def _rope_tables(length):
    pos = jnp.arange(length)
    row = (pos // GRID_W).astype(F32)
    col = (pos % GRID_W).astype(F32)
    inv = 1.0 / (ROPE_THETA ** (jnp.arange(ROPE_HALF, dtype=F32) / ROPE_HALF))
    ang_r = row[:, None] * inv[None, :]
    ang_c = col[:, None] * inv[None, :]
    ang = jnp.concatenate([ang_r, ang_r, ang_c, ang_c], axis=-1)
    cos = jnp.tile(jnp.cos(ang).astype(F32), (1, HEAD_COLS // ATTN_QK_DIM))
    sin = jnp.tile(jnp.sin(ang).astype(F32), (1, HEAD_COLS // ATTN_QK_DIM))
    first_half = (jnp.arange(HEAD_COLS) % (2 * ROPE_HALF)) < ROPE_HALF
    sin_from_upper = jnp.where(first_half, -sin, 0.0)
    sin_from_lower = jnp.where(first_half, 0.0, sin)
    return cos, sin_from_upper, sin_from_lower


def _inproj_kernel(*refs, rope, kv_only):
    x_ref, sh_ref, sc_ref, g_ref, w_ref = refs[:5]
    refs = refs[5:]
    if rope:
        cos_ref, sup_ref, slo_ref = refs[:3]
        refs = refs[3:]
    if kv_only:
        k_ref, vt_ref = refs
    else:
        q_ref, k_ref, vt_ref, cp_ref = refs

    h = _rms(x_ref[0]) * (g_ref[...] * (1.0 + sc_ref[...])) + sh_ref[...]
    hb = h.astype(BF16)

    def rotate(t, scale):
        if not rope:
            return t if scale == 1.0 else t * scale
        out = []
        for j in range(Q_COLS // HEAD_COLS):
            tj = t[:, j * HEAD_COLS:(j + 1) * HEAD_COLS]
            rj = (tj * cos_ref[...]
                  + pltpu.roll(tj, HEAD_COLS - ROPE_HALF, 1) * sup_ref[...]
                  + pltpu.roll(tj, ROPE_HALF, 1) * slo_ref[...])
            out.append(rj if scale == 1.0 else rj * scale)
        return jnp.concatenate(out, axis=1)

    if not kv_only:
        q = _dot(hb, w_ref[:, 0:Q_COLS])
        q_ref[0] = rotate(q, Q_PRESCALE).astype(BF16)
    k = _dot(hb, w_ref[:, Q_COLS:2 * Q_COLS])
    k_ref[0] = rotate(k, 1.0).astype(BF16)
    v = _dot(hb, w_ref[:, 2 * Q_COLS:3 * Q_COLS])
    vt_ref[0] = v.T.astype(BF16)
    if not kv_only:
        cp_ref[0] = _dot(hb, w_ref[:, 3 * Q_COLS:IN_COLS]).astype(BF16)


def _inproj(x, mod, mod_row, g_norm1, w_in, tables, *, layer, tm, kv_only=False):
    b, length, _ = x.shape
    rope = tables is not None
    in_specs = [
        pl.BlockSpec((1, tm, D_MODEL), lambda bi, i: (bi, i, 0)),
        _mod_block(layer, 0, mod_row),
        _mod_block(layer, 1, mod_row),
        _layer_block((1, D_MODEL), layer),
        _layer_block((D_MODEL, IN_COLS), layer, pipeline_mode=pl.Buffered(1)),
    ]
    args = [x, mod, mod, g_norm1, w_in]
    if rope:
        in_specs += [pl.BlockSpec((tm, HEAD_COLS), lambda bi, i: (i, 0))] * 3
        args += list(tables)
    row_spec = lambda cols: pl.BlockSpec((1, tm, cols), lambda bi, i: (bi, i, 0))
    row_shape = lambda cols: jax.ShapeDtypeStruct((b, length, cols), BF16)
    vt_spec = pl.BlockSpec((1, V_COLS, tm), lambda bi, i: (bi, 0, i))
    vt_shape = jax.ShapeDtypeStruct((b, V_COLS, length), BF16)
    if kv_only:
        out_specs = [row_spec(Q_COLS), vt_spec]
        out_shape = [row_shape(Q_COLS), vt_shape]
    else:
        out_specs = [row_spec(Q_COLS), row_spec(Q_COLS), vt_spec, row_spec(CP_COLS)]
        out_shape = [row_shape(Q_COLS), row_shape(Q_COLS), vt_shape, row_shape(CP_COLS)]
    return pl.pallas_call(
        functools.partial(_inproj_kernel, rope=rope, kv_only=kv_only),
        grid=(b, length // tm),
        in_specs=in_specs,
        out_specs=out_specs,
        out_shape=out_shape,
        compiler_params=_params(),
        name="inproj_lat" if rope else "inproj_ctx",
    )(*args)


def _attn_kernel(*refs, key_lens, lam_init, tq, heads):
    n_src = len(key_lens)
    lam_ref, g_ref, q_ref = refs[:3]
    k_refs = refs[3:3 + n_src]
    vt_refs = refs[3 + n_src:3 + 2 * n_src]
    o_ref = refs[3 + 2 * n_src]
    scratch = refs[4 + 2 * n_src:]
    k_all, vt_all = scratch[0:KV_SETS], scratch[KV_SETS:2 * KV_SETS]
    scratch = scratch[2 * KV_SETS:]
    s_ref = (scratch[0:2], scratch[2:4])
    p_ref = (scratch[4:6], scratch[6:8])

    lq = lam_ref[...]
    lam = (jnp.exp(jnp.sum(lq[0:1] * lq[1:2], axis=1, keepdims=True))
           - jnp.exp(jnp.sum(lq[2:3] * lq[3:4], axis=1, keepdims=True)) + lam_init)
    lane = lax.broadcasted_iota(jnp.int32, (tq, HEAD_COLS), 1)
    n_q = q_ref.shape[1] // tq
    lk_total = sum(key_lens)
    tiles = [(head, i) for head in range(heads) for i in range(n_q)]
    head_cols = lambda head: slice(head * HEAD_COLS, (head + 1) * HEAD_COLS)
    q_rows = lambda i: slice(i * tq, (i + 1) * tq)

    def load_head(head):
        off = 0
        for k_ref, vt_ref, lk in zip(k_refs, vt_refs, key_lens):
            k_all[head % KV_SETS][off:off + lk, :] = k_ref[0, :, head_cols(head)]
            vt_all[head % KV_SETS][0:ATTN_V_DIM, off:off + lk] = vt_ref[0, head_cols(head), :]
            off += lk
        vt_all[head % KV_SETS][ATTN_V_DIM:, :] = jnp.ones((ONES_ROWS, lk_total), BF16)

    def scores(t):
        head, i = tiles[t]
        if i == 0:
            load_head(head)
        q = q_ref[0, q_rows(i), head_cols(head)]
        zero = jnp.zeros_like(q)
        maxima = []
        for c, keep in enumerate((lane < ATTN_QK_DIM, lane >= ATTN_QK_DIM)):
            s = lax.dot_general(k_all[head % KV_SETS][...], jnp.where(keep, q, zero),
                                (((1,), (1,)), ((), ())), preferred_element_type=F32)
            s_ref[t % 2][c][...] = s
            maxima.append(jnp.max(s, axis=0, keepdims=True))
        return tuple(maxima)

    def softmax(t, maxima):
        for c in range(2):
            p_ref[t % 2][c][...] = jnp.exp2(s_ref[t % 2][c][...] - maxima[c]).astype(BF16)

    def values(t):
        head, i = tiles[t]
        outs = []
        for c in range(2):
            av = _dot(vt_all[head % KV_SETS][...], p_ref[t % 2][c][...])
            outs.append(av[0:ATTN_V_DIM] / av[ATTN_V_DIM:ATTN_V_DIM + 1])
        o_t = outs[0] - lam * outs[1]
        y = o_t * lax.rsqrt(jnp.mean(o_t * o_t, axis=0, keepdims=True) + EPS)
        o = (y.T * g_ref[...]) * (1.0 - lam_init)
        o_ref[0, q_rows(i), head_cols(head)] = o.astype(BF16)

    n = len(tiles)
    maxima = {0: scores(0)}
    if n > 1:
        maxima[1] = scores(1)
    softmax(0, maxima[0])
    for t in range(n):
        if t + 2 < n:
            maxima[t + 2] = scores(t + 2)
        if t + 1 < n:
            softmax(t + 1, maxima[t + 1])
        values(t)


def _attention(lam_rows, g_subln, q, ks, vts, *, layer, lam_init, tq, heads):
    b, lq, _ = q.shape
    key_lens = tuple(k.shape[1] for k in ks)
    lk_total = sum(key_lens)
    cols = heads * HEAD_COLS
    head_rows = lambda length: pl.BlockSpec((1, length, cols), lambda bi, h: (bi, 0, h))
    in_specs = [
        _layer_block((4, ATTN_QK_DIM), layer),
        _layer_block((1, ATTN_V_DIM), layer),
        head_rows(lq),
    ]
    in_specs += [head_rows(lk) for lk in key_lens]
    in_specs += [pl.BlockSpec((1, cols, lk), lambda bi, h: (bi, h, 0)) for lk in key_lens]
    return pl.pallas_call(
        functools.partial(_attn_kernel, key_lens=key_lens, lam_init=lam_init, tq=tq, heads=heads),
        grid=(b, ATTN_HEADS // heads),
        in_specs=in_specs,
        out_specs=head_rows(lq),
        out_shape=jax.ShapeDtypeStruct((b, lq, V_COLS), BF16),
        scratch_shapes=[*[pltpu.VMEM((lk_total, HEAD_COLS), BF16)] * KV_SETS,
                        *[pltpu.VMEM((ATTN_V_DIM + ONES_ROWS, lk_total), BF16)] * KV_SETS,
                        *[pltpu.VMEM((lk_total, tq), F32)] * 4, *[pltpu.VMEM((lk_total, tq), BF16)] * 4],
        compiler_params=_params(),
        name="diff_attn_lat" if len(ks) == 2 else "diff_attn_ctx",
    )(lam_rows, g_subln, q, *ks, *vts)


def _mixffn_kernel(x_ref, o_ref, cp_prev_ref, cp_ref, cp_next_ref, inv_cnt_ref, gt1_ref, sh2_ref, sc2_ref, gt2_ref,
                   g2_ref, wconv_ref, wpool_ref, spool_ref, wout_ref, wgu_ref, wd_ref, gfin_ref,
                   out_ref, *acc_refs, length, tm, final):
    i = pl.program_id(1)
    n_tiles = length // tm
    n_sub = len(acc_refs)
    ts = tm // n_sub
    rows = ts + 2 * HALO
    core = lambda a: a[HALO:HALO + ts]

    def shifted(a, d):
        return pltpu.roll(a, d % rows, 0)

    def sub_tile(k):
        r0 = k * ts
        tile_rows = slice(r0, r0 + ts)
        if k == 0:
            before = jnp.where(i > 0, cp_prev_ref[0].astype(F32), 0.0)
        else:
            before = cp_ref[0, r0 - HALO:r0, :].astype(F32)
        if k == n_sub - 1:
            after = jnp.where(i < n_tiles - 1, cp_next_ref[0].astype(F32), 0.0)
        else:
            after = cp_ref[0, r0 + ts:r0 + ts + HALO, :].astype(F32)
        cur = cp_ref[0, tile_rows, :].astype(F32)
        ext = jnp.concatenate([before, cur, after], axis=0)

        u = ext[:, CONV_WIDTH:2 * CONV_WIDTH] * ext[:, 2 * CONV_WIDTH:3 * CONV_WIDTH]
        wc = wconv_ref[...]
        y = shifted(u, 1) * wc[0:1] + u * wc[1:2] + shifted(u, -1) * wc[2:3]
        conv = cur[:, 0:CONV_WIDTH] * core(y)

        lane = lax.broadcasted_iota(jnp.int32, (ts, V7X_LANES), 1)
        pooled = []
        for tile in range(POOL_WIDTH // V7X_LANES):
            lanes = slice(tile * V7X_LANES, (tile + 1) * V7X_LANES)
            p = ext[:, 3 * CONV_WIDTH + tile * V7X_LANES:3 * CONV_WIDTH + (tile + 1) * V7X_LANES]
            sums = {2: p + shifted(p, 1)}
            for w in (4, 8, 16):
                sums[w] = shifted(sums[w // 2], w // 4) + shifted(sums[w // 2], -(w // 4))
            w_lo, w_hi = POOL_WINDOWS[2 * tile], POOL_WINDOWS[2 * tile + 1]
            win_sum = jnp.where(lane < POOL_GROUP_DIM, core(sums[w_lo]), core(sums[w_hi]))
            pooled.append(win_sum * inv_cnt_ref[tile_rows, lanes] - core(p))
        pooled = jnp.concatenate(pooled, axis=1).astype(BF16)
        pool_y = _dot(pooled, wpool_ref[...]) * spool_ref[...]

        wo = wout_ref
        mix = (_dot(o_ref[0, tile_rows, :], wo[0:V_COLS, :])
               + _dot(conv.astype(BF16), wo[V_COLS:V_COLS + CONV_WIDTH, :])
               + _dot(pool_y.astype(BF16), wo[V_COLS + CONV_WIDTH:D_MODEL, :]))
        x1 = x_ref[0, tile_rows, :] + gt1_ref[...] * mix
        out_ref[0, tile_rows, :] = x1
        h2 = (_rms(x1) * (g2_ref[...] * (1.0 + sc2_ref[...])) + sh2_ref[...]).astype(BF16)

        acc_ref = acc_refs[k]
        for c in range(N_FFN_CHUNKS):
            gate = _dot(h2, wgu_ref[:, c * FFN_CHUNK:(c + 1) * FFN_CHUNK])
            up = _dot(h2, wgu_ref[:, FFN_HIDDEN + c * FFN_CHUNK:FFN_HIDDEN + (c + 1) * FFN_CHUNK])
            act = (gate * jax.nn.sigmoid(gate) * up).astype(BF16)
            part = _dot(act, wd_ref[c * FFN_CHUNK:(c + 1) * FFN_CHUNK, :])
            if c == 0:
                acc_ref[...] = part
            else:
                acc_ref[...] += part

        x2 = out_ref[0, tile_rows, :] + gt2_ref[...] * acc_ref[...]
        if final:
            x2 = _rms(x2) * gfin_ref[...]
        out_ref[0, tile_rows, :] = x2

    for k in range(n_sub):
        sub_tile(k)


def _pool_inv_counts(length):
    t = jnp.arange(length)[:, None]
    half = jnp.repeat(jnp.array(POOL_WINDOWS, dtype=jnp.int32) // 2, POOL_GROUP_DIM)[None, :]
    cnt = jnp.minimum(t + half - 1, length - 1) - jnp.maximum(t - half, 0) + 1
    return 1.0 / cnt.astype(F32)


def _mixffn(x, o, cp, mod, mod_row, g_norm2, w_conv, w_pool_bd, s_pool, w_out, w_gate_up, w_down, g_final,
            *, layer, tm, n_sub, final):
    b, length, _ = x.shape
    n_tiles = length // tm
    halo_per_tile = tm // HALO
    n_halo_blocks = length // HALO
    weight = functools.partial(_layer_block, layer=layer, pipeline_mode=pl.Buffered(1))
    in_specs = [
        pl.BlockSpec((1, tm, D_MODEL), lambda bi, i: (bi, i, 0)),
        pl.BlockSpec((1, tm, V_COLS), lambda bi, i: (bi, i, 0)),
        pl.BlockSpec((1, HALO, CP_COLS), lambda bi, i: (bi, jnp.maximum(i * halo_per_tile - 1, 0), 0)),
        pl.BlockSpec((1, tm, CP_COLS), lambda bi, i: (bi, i, 0)),
        pl.BlockSpec((1, HALO, CP_COLS),
                     lambda bi, i: (bi, jnp.minimum((i + 1) * halo_per_tile, n_halo_blocks - 1), 0)),
        pl.BlockSpec((tm, POOL_WIDTH), lambda bi, i: (i, 0)),
    ] + [_mod_block(layer, which, mod_row) for which in (2, 3, 4, 5)] + [
        _layer_block((1, D_MODEL), layer),
        _layer_block((3, CONV_WIDTH), layer),
        weight((POOL_WIDTH, POOL_WIDTH)),
        _layer_block((1, POOL_WIDTH), layer),
        weight((D_MODEL, D_MODEL)),
        weight((D_MODEL, 2 * FFN_HIDDEN)),
        weight((FFN_HIDDEN, D_MODEL)),
        pl.BlockSpec((1, D_MODEL), lambda bi, i: (0, 0)),
    ]
    return pl.pallas_call(
        functools.partial(_mixffn_kernel, length=length, tm=tm, final=final),
        grid=(b, n_tiles),
        in_specs=in_specs,
        out_specs=pl.BlockSpec((1, tm, D_MODEL), lambda bi, i: (bi, i, 0)),
        out_shape=jax.ShapeDtypeStruct((b, length, D_MODEL), F32),
        scratch_shapes=[pltpu.VMEM((tm // n_sub, D_MODEL), F32)] * n_sub,
        compiler_params=_params(),
        name="mix_ffn_lat" if mod_row is None else "mix_ffn_ctx",
    )(x, o, cp, cp, cp, _pool_inv_counts(length), mod, mod, mod, mod, g_norm2, w_conv, w_pool_bd, s_pool, w_out,
      w_gate_up, w_down, g_final)


@jax.jit
def _forward(x, c, ctx, c_ctx, w_ada, b_ada, g_norm1, w_in, lam_q1, lam_k1, lam_q2, lam_k2,
             g_subln, w_conv, w_pool, s_pool, w_out, g_norm2, w_gate_up, w_down, g_final):
    batch, seq_len, _ = x.shape
    ctx_len = ctx.shape[1]
    ctx_row = batch
    assert batch < MOD_ROWS
    tiles = _choose_tiles(seq_len, ctx_len)

    cc = jnp.concatenate([c, c_ctx[None, :], jnp.zeros((MOD_ROWS - batch - 1, D_MODEL), F32)], axis=0)
    mod = _ada_rows(cc, w_ada, b_ada).reshape(DEPTH, MOD_ROWS, N_MOD, 1, D_MODEL)

    w_in_b = w_in.astype(BF16)
    w_out_b = w_out.astype(BF16)
    w_gu_b = w_gate_up.astype(BF16)
    w_down_b = w_down.astype(BF16)
    eye = jnp.eye(POOL_GROUPS, dtype=F32)
    w_pool_bd = (w_pool[:, :, :, None, :] * eye[None, :, None, :, None]).reshape(
        DEPTH, POOL_WIDTH, POOL_WIDTH).astype(BF16)
    lam_rows = jnp.stack([lam_q1, lam_k1, lam_q2, lam_k2], axis=1)
    g1 = g_norm1[:, None, :]
    g2 = g_norm2[:, None, :]
    gs = g_subln[:, None, :]
    sp = s_pool[:, None, :]
    gf = g_final[None, :]
    tables = _rope_tables(seq_len)

    x_lat, x_ctx = x, ctx
    for l in range(DEPTH):
        last = l == DEPTH - 1
        lam_init = 0.8 - 0.6 * math.exp(-0.3 * l)
        ctx_proj = _inproj(x_ctx, mod, ctx_row, g1, w_in_b, None, layer=l, tm=tiles.tm_ctx, kv_only=last)
        q_l, k_l, vt_l, cp_l = _inproj(x_lat, mod, None, g1, w_in_b, tables, layer=l, tm=tiles.tm_in)
        k_c, vt_c = ctx_proj[-3:-1] if not last else ctx_proj
        o_l = _attention(lam_rows, gs, q_l, [k_c, k_l], [vt_c, vt_l], layer=l, lam_init=lam_init,
                         tq=tiles.tq_lat, heads=tiles.heads_lat)
        mix_args = (g2, w_conv, w_pool_bd, sp, w_out_b, w_gu_b, w_down_b, gf)
        x_lat = _mixffn(x_lat, o_l, cp_l, mod, None, *mix_args, layer=l, tm=tiles.tm_lat, n_sub=tiles.sub_lat,
                        final=last)
        if not last:
            q_c, _, _, cp_c = ctx_proj
            o_c = _attention(lam_rows, gs, q_c, [k_c], [vt_c], layer=l, lam_init=lam_init,
                             tq=tiles.tq_ctx, heads=ATTN_HEADS)
            x_ctx = _mixffn(x_ctx, o_c, cp_c, mod, ctx_row, *mix_args, layer=l, tm=tiles.tm_ctx, n_sub=1,
                            final=False)
    return x_lat


def kernel(x, c, ctx, c_ctx, w_ada, b_ada, g_norm1, w_in, lam_q1, lam_k1, lam_q2, lam_k2, g_subln, w_conv,
           w_pool, s_pool, w_out, g_norm2, w_gate_up, w_down, g_final):
    return _forward(x, c, ctx, c_ctx, w_ada, b_ada, g_norm1, w_in, lam_q1, lam_k1, lam_q2, lam_k2, g_subln,
                    w_conv, w_pool, s_pool, w_out, g_norm2, w_gate_up, w_down, g_final)
```

```python
import functools
import math
from typing import NamedTuple

import jax
import jax.numpy as jnp
from jax import lax
from jax.experimental import pallas as pl
from jax.experimental.pallas import tpu as pltpu

D_MODEL = 1024
DEPTH = 4
GRID_W = 64
N_MOD = 6
ATTN_HEADS = 4
ATTN_QK_DIM = 64
ATTN_V_DIM = 2 * ATTN_QK_DIM
HEAD_COLS = 2 * ATTN_QK_DIM
Q_COLS = ATTN_HEADS * HEAD_COLS
V_COLS = ATTN_HEADS * ATTN_V_DIM
ATTN_SCALE = ATTN_QK_DIM ** -0.5
ROPE_THETA = 10000.0
ROPE_HALF = ATTN_QK_DIM // 4
CONV_WIDTH = D_MODEL // 4
POOL_WINDOWS = (2, 4, 8, 16)
POOL_GROUPS = len(POOL_WINDOWS)
POOL_WIDTH = D_MODEL - V_COLS - CONV_WIDTH
POOL_GROUP_DIM = POOL_WIDTH // POOL_GROUPS
CP_COLS = 3 * CONV_WIDTH + POOL_WIDTH
IN_COLS = 3 * Q_COLS + CP_COLS
FFN_HIDDEN = -(-(8 * D_MODEL) // (3 * 256)) * 256
EPS = 1e-6

V7X_LANES = 128
V7X_MXU_COLS = 256
V7X_BF16_SUBLANES = 16
V7X_VMEM_LIMIT_BYTES = 56 * 1024 * 1024

MOD_ROWS = 40
ONES_ROWS = V7X_BF16_SUBLANES
KV_SETS = 3
HALO = V7X_BF16_SUBLANES
FFN_CHUNK = V7X_MXU_COLS
N_FFN_CHUNKS = FFN_HIDDEN // FFN_CHUNK
ADA_COL_BLOCK = 1536
Q_PRESCALE = ATTN_SCALE * math.log2(math.e)

F32 = jnp.float32
BF16 = jnp.bfloat16


class Tiles(NamedTuple):
    tm_in: int
    tm_lat: int
    tm_ctx: int
    tq_lat: int
    tq_ctx: int


def _choose_tiles(seq_len, ctx_len):
    return Tiles(tm_in=min(1024, seq_len), tm_lat=min(512, seq_len), tm_ctx=min(512, ctx_len),
                 tq_lat=min(V7X_MXU_COLS, seq_len), tq_ctx=min(V7X_MXU_COLS, ctx_len))


def _params(grid_rank=2):
    return pltpu.CompilerParams(dimension_semantics=("arbitrary",) * grid_rank,
                                vmem_limit_bytes=V7X_VMEM_LIMIT_BYTES)


def _dot(a, b):
    return jnp.dot(a, b, preferred_element_type=F32)


def _rms(x):
    return x * lax.rsqrt(jnp.mean(x * x, axis=-1, keepdims=True) + EPS)


def _layer_block(shape, layer, **kwargs):
    return pl.BlockSpec((None,) + tuple(shape), lambda bi, i: (layer,) + (0,) * len(shape), **kwargs)


def _mod_block(layer, which, mod_row):
    row = (lambda bi: bi) if mod_row is None else (lambda bi: mod_row)
    return pl.BlockSpec((None, None, None, 1, D_MODEL), lambda bi, i: (layer, row(bi), which, 0, 0))


def _ada_kernel(cc_ref, w_ref, b_ref, out_ref):
    cc = cc_ref[...]
    s = cc * jax.nn.sigmoid(cc)
    out_ref[0] = _dot(s.astype(BF16), w_ref[0].astype(BF16)) + b_ref[0]


def _ada_rows(cc, w_ada, b_ada):
    n_col = N_MOD * D_MODEL
    return pl.pallas_call(
        _ada_kernel,
        grid=(DEPTH, n_col // ADA_COL_BLOCK),
        in_specs=[
            pl.BlockSpec((MOD_ROWS, D_MODEL), lambda l, j: (0, 0)),
            pl.BlockSpec((1, D_MODEL, ADA_COL_BLOCK), lambda l, j: (l, 0, j)),
            pl.BlockSpec((1, 1, ADA_COL_BLOCK), lambda l, j: (l, 0, j)),
        ],
        out_specs=pl.BlockSpec((1, MOD_ROWS, ADA_COL_BLOCK), lambda l, j: (l, 0, j)),
        out_shape=jax.ShapeDtypeStruct((DEPTH, MOD_ROWS, n_col), F32),
        compiler_params=_params(),
        name="ada_rows",
    )(cc, w_ada, b_ada.reshape(DEPTH, 1, n_col))


def _rope_tables(length):
    pos = jnp.arange(length)
    row = (pos // GRID_W).astype(F32)
    col = (pos % GRID_W).astype(F32)
    inv = 1.0 / (ROPE_THETA ** (jnp.arange(ROPE_HALF, dtype=F32) / ROPE_HALF))
    ang_r = row[:, None] * inv[None, :]
    ang_c = col[:, None] * inv[None, :]
    ang = jnp.concatenate([ang_r, ang_r, ang_c, ang_c], axis=-1)
    cos = jnp.tile(jnp.cos(ang).astype(F32), (1, HEAD_COLS // ATTN_QK_DIM))
    sin = jnp.tile(jnp.sin(ang).astype(F32), (1, HEAD_COLS // ATTN_QK_DIM))
    first_half = (jnp.arange(HEAD_COLS) % (2 * ROPE_HALF)) < ROPE_HALF
    sin_from_upper = jnp.where(first_half, -sin, 0.0)
    sin_from_lower = jnp.where(first_half, 0.0, sin)
    return cos, sin_from_upper, sin_from_lower


def _inproj_kernel(*refs, rope, kv_only):
    x_ref, sh_ref, sc_ref, g_ref, w_ref = refs[:5]
    refs = refs[5:]
    if rope:
        cos_ref, sup_ref, slo_ref = refs[:3]
        refs = refs[3:]
    if kv_only:
        k_ref, vt_ref = refs
    else:
        q_ref, k_ref, vt_ref, cp_ref = refs

    h = _rms(x_ref[0]) * (g_ref[...] * (1.0 + sc_ref[...])) + sh_ref[...]
    hb = h.astype(BF16)

    def rotate(t, scale):
        if not rope:
            return t if scale == 1.0 else t * scale
        out = []
        for j in range(Q_COLS // HEAD_COLS):
            tj = t[:, j * HEAD_COLS:(j + 1) * HEAD_COLS]
            rj = (tj * cos_ref[...]
                  + pltpu.roll(tj, HEAD_COLS - ROPE_HALF, 1) * sup_ref[...]
                  + pltpu.roll(tj, ROPE_HALF, 1) * slo_ref[...])
            out.append(rj if scale == 1.0 else rj * scale)
        return jnp.concatenate(out, axis=1)

    if not kv_only:
        q = _dot(hb, w_ref[:, 0:Q_COLS])
        q_ref[0] = rotate(q, Q_PRESCALE).astype(BF16)
    k = _dot(hb, w_ref[:, Q_COLS:2 * Q_COLS])
    k_ref[0] = rotate(k, 1.0).astype(BF16)
    v = _dot(hb, w_ref[:, 2 * Q_COLS:3 * Q_COLS])
    vt_ref[0] = v.T.astype(BF16)
    if not kv_only:
        cp_ref[0] = _dot(hb, w_ref[:, 3 * Q_COLS:IN_COLS]).astype(BF16)


def _inproj(x, mod, mod_row, g_norm1, w_in, tables, *, layer, tm, kv_only=False):
    b, length, _ = x.shape
    rope = tables is not None
    in_specs = [
        pl.BlockSpec((1, tm, D_MODEL), lambda bi, i: (bi, i, 0)),
        _mod_block(layer, 0, mod_row),
        _mod_block(layer, 1, mod_row),
        _layer_block((1, D_MODEL), layer),
        _layer_block((D_MODEL, IN_COLS), layer, pipeline_mode=pl.Buffered(1)),
    ]
    args = [x, mod, mod, g_norm1, w_in]
    if rope:
        in_specs += [pl.BlockSpec((tm, HEAD_COLS), lambda bi, i: (i, 0))] * 3
        args += list(tables)
    row_spec = lambda cols: pl.BlockSpec((1, tm, cols), lambda bi, i: (bi, i, 0))
    row_shape = lambda cols: jax.ShapeDtypeStruct((b, length, cols), BF16)
    vt_spec = pl.BlockSpec((1, V_COLS, tm), lambda bi, i: (bi, 0, i))
    vt_shape = jax.ShapeDtypeStruct((b, V_COLS, length), BF16)
    if kv_only:
        out_specs = [row_spec(Q_COLS), vt_spec]
        out_shape = [row_shape(Q_COLS), vt_shape]
    else:
        out_specs = [row_spec(Q_COLS), row_spec(Q_COLS), vt_spec, row_spec(CP_COLS)]
        out_shape = [row_shape(Q_COLS), row_shape(Q_COLS), vt_shape, row_shape(CP_COLS)]
    return pl.pallas_call(
        functools.partial(_inproj_kernel, rope=rope, kv_only=kv_only),
        grid=(b, length // tm),
        in_specs=in_specs,
        out_specs=out_specs,
        out_shape=out_shape,
        compiler_params=_params(),
        name="inproj_lat" if rope else "inproj_ctx",
    )(*args)


def _attn_kernel(*refs, key_lens, lam_init, tq, heads):
    n_src = len(key_lens)
    lam_ref, g_ref, q_ref = refs[:3]
    k_refs = refs[3:3 + n_src]
    vt_refs = refs[3 + n_src:3 + 2 * n_src]
    o_ref = refs[3 + 2 * n_src]
    scratch = refs[4 + 2 * n_src:]
    k_all, vt_all = scratch[0:KV_SETS], scratch[KV_SETS:2 * KV_SETS]
    scratch = scratch[2 * KV_SETS:]
    s_ref = (scratch[0:2], scratch[2:4])
    p_ref = (scratch[4:6], scratch[6:8])

    lq = lam_ref[...]
    lam = (jnp.exp(jnp.sum(lq[0:1] * lq[1:2], axis=1, keepdims=True))
           - jnp.exp(jnp.sum(lq[2:3] * lq[3:4], axis=1, keepdims=True)) + lam_init)
    lane = lax.broadcasted_iota(jnp.int32, (tq, HEAD_COLS), 1)
    n_q = q_ref.shape[1] // tq
    lk_total = sum(key_lens)
    tiles = [(head, i) for head in range(heads) for i in range(n_q)]
    head_cols = lambda head: slice(head * HEAD_COLS, (head + 1) * HEAD_COLS)
    q_rows = lambda i: slice(i * tq, (i + 1) * tq)

    def load_head(head):
        off = 0
        for k_ref, vt_ref, lk in zip(k_refs, vt_refs, key_lens):
            k_all[head % KV_SETS][off:off + lk, :] = k_ref[0, :, head_cols(head)]
            vt_all[head % KV_SETS][0:ATTN_V_DIM, off:off + lk] = vt_ref[0, head_cols(head), :]
            off += lk
        vt_all[head % KV_SETS][ATTN_V_DIM:, :] = jnp.ones((ONES_ROWS, lk_total), BF16)

    def scores(t):
        head, i = tiles[t]
        if i == 0:
            load_head(head)
        q = q_ref[0, q_rows(i), head_cols(head)]
        zero = jnp.zeros_like(q)
        maxima = []
        for c, keep in enumerate((lane < ATTN_QK_DIM, lane >= ATTN_QK_DIM)):
            s = lax.dot_general(k_all[head % KV_SETS][...], jnp.where(keep, q, zero),
                                (((1,), (1,)), ((), ())), preferred_element_type=F32)
            s_ref[t % 2][c][...] = s
            maxima.append(jnp.max(s, axis=0, keepdims=True))
        return tuple(maxima)

    def softmax(t, maxima):
        for c in range(2):
            p_ref[t % 2][c][...] = jnp.exp2(s_ref[t % 2][c][...] - maxima[c]).astype(BF16)

    def values(t):
        head, i = tiles[t]
        outs = []
        for c in range(2):
            av = _dot(vt_all[head % KV_SETS][...], p_ref[t % 2][c][...])
            outs.append(av[0:ATTN_V_DIM] / av[ATTN_V_DIM:ATTN_V_DIM + 1])
        o_t = outs[0] - lam * outs[1]
        y = o_t * lax.rsqrt(jnp.mean(o_t * o_t, axis=0, keepdims=True) + EPS)
        o = (y.T * g_ref[...]) * (1.0 - lam_init)
        o_ref[0, q_rows(i), head_cols(head)] = o.astype(BF16)

    n = len(tiles)
    maxima = {0: scores(0)}
    if n > 1:
        maxima[1] = scores(1)
    softmax(0, maxima[0])
    for t in range(n):
        if t + 2 < n:
            maxima[t + 2] = scores(t + 2)
        if t + 1 < n:
            softmax(t + 1, maxima[t + 1])
        values(t)


def _attention(lam_rows, g_subln, q, ks, vts, *, layer, lam_init, tq, heads):
    b, lq, _ = q.shape
    key_lens = tuple(k.shape[1] for k in ks)
    lk_total = sum(key_lens)
    cols = heads * HEAD_COLS
    head_rows = lambda length: pl.BlockSpec((1, length, cols), lambda bi, h: (bi, 0, h))
    in_specs = [
        _layer_block((4, ATTN_QK_DIM), layer),
        _layer_block((1, ATTN_V_DIM), layer),
        head_rows(lq),
    ]
    in_specs += [head_rows(lk) for lk in key_lens]
    in_specs += [pl.BlockSpec((1, cols, lk), lambda bi, h: (bi, h, 0)) for lk in key_lens]
    return pl.pallas_call(
        functools.partial(_attn_kernel, key_lens=key_lens, lam_init=lam_init, tq=tq, heads=heads),
        grid=(b, ATTN_HEADS // heads),
        in_specs=in_specs,
        out_specs=head_rows(lq),
        out_shape=jax.ShapeDtypeStruct((b, lq, V_COLS), BF16),
        scratch_shapes=[*[pltpu.VMEM((lk_total, HEAD_COLS), BF16)] * KV_SETS,
                        *[pltpu.VMEM((ATTN_V_DIM + ONES_ROWS, lk_total), BF16)] * KV_SETS,
                        *[pltpu.VMEM((lk_total, tq), F32)] * 4, *[pltpu.VMEM((lk_total, tq), BF16)] * 4],
        compiler_params=_params(),
        name="diff_attn_lat" if len(ks) == 2 else "diff_attn_ctx",
    )(lam_rows, g_subln, q, *ks, *vts)


def _mixffn_kernel(x_ref, o_ref, cp_prev_ref, cp_ref, cp_next_ref, inv_cnt_ref, gt1_ref, sh2_ref, sc2_ref, gt2_ref,
                   g2_ref, wconv_ref, wpool_ref, spool_ref, wout_ref, wgu_ref, wd_ref, gfin_ref,
                   out_ref, acc_ref, *, length, tm, final):
    i = pl.program_id(1)
    n_tiles = length // tm
    rows = tm + 2 * HALO
    core = lambda a: a[HALO:HALO + tm]

    def shifted(a, d):
        return pltpu.roll(a, d % rows, 0)

    before = jnp.where(i > 0, cp_prev_ref[0].astype(F32), 0.0)
    after = jnp.where(i < n_tiles - 1, cp_next_ref[0].astype(F32), 0.0)
    cur = cp_ref[0].astype(F32)
    ext = jnp.concatenate([before, cur, after], axis=0)

    u = ext[:, CONV_WIDTH:2 * CONV_WIDTH] * ext[:, 2 * CONV_WIDTH:3 * CONV_WIDTH]
    wc = wconv_ref[...]
    y = shifted(u, 1) * wc[0:1] + u * wc[1:2] + shifted(u, -1) * wc[2:3]
    conv = cur[:, 0:CONV_WIDTH] * core(y)

    lane = lax.broadcasted_iota(jnp.int32, (tm, V7X_LANES), 1)
    pooled = []
    for tile in range(POOL_WIDTH // V7X_LANES):
        lanes = slice(tile * V7X_LANES, (tile + 1) * V7X_LANES)
        p = ext[:, 3 * CONV_WIDTH + tile * V7X_LANES:3 * CONV_WIDTH + (tile + 1) * V7X_LANES]
        sums = {2: p + shifted(p, 1)}
        for w in (4, 8, 16):
            sums[w] = shifted(sums[w // 2], w // 4) + shifted(sums[w // 2], -(w // 4))
        w_lo, w_hi = POOL_WINDOWS[2 * tile], POOL_WINDOWS[2 * tile + 1]
        win_sum = jnp.where(lane < POOL_GROUP_DIM, core(sums[w_lo]), core(sums[w_hi]))
        pooled.append(win_sum * inv_cnt_ref[:, lanes] - core(p))
    pooled = jnp.concatenate(pooled, axis=1).astype(BF16)
    pool_y = _dot(pooled, wpool_ref[...]) * spool_ref[...]

    mixed = jnp.concatenate([o_ref[0], conv.astype(BF16), pool_y.astype(BF16)], axis=1)
    x1 = x_ref[0] + gt1_ref[...] * _dot(mixed, wout_ref[...])
    out_ref[0] = x1
    h2 = (_rms(x1) * (g2_ref[...] * (1.0 + sc2_ref[...])) + sh2_ref[...]).astype(BF16)

    for c in range(N_FFN_CHUNKS):
        gate = _dot(h2, wgu_ref[:, c * FFN_CHUNK:(c + 1) * FFN_CHUNK])
        up = _dot(h2, wgu_ref[:, FFN_HIDDEN + c * FFN_CHUNK:FFN_HIDDEN + (c + 1) * FFN_CHUNK])
        act = (gate * jax.nn.sigmoid(gate) * up).astype(BF16)
        part = _dot(act, wd_ref[c * FFN_CHUNK:(c + 1) * FFN_CHUNK, :])
        if c == 0:
            acc_ref[...] = part
        else:
            acc_ref[...] += part

    x2 = out_ref[0] + gt2_ref[...] * acc_ref[...]
    if final:
        x2 = _rms(x2) * gfin_ref[...]
    out_ref[0] = x2


def _pool_inv_counts(length):
    t = jnp.arange(length)[:, None]
    half = jnp.repeat(jnp.array(POOL_WINDOWS, dtype=jnp.int32) // 2, POOL_GROUP_DIM)[None, :]
    cnt = jnp.minimum(t + half - 1, length - 1) - jnp.maximum(t - half, 0) + 1
    return 1.0 / cnt.astype(F32)


def _mixffn(x, o, cp, mod, mod_row, g_norm2, w_conv, w_pool_bd, s_pool, w_out, w_gate_up, w_down, g_final,
            *, layer, tm, final):
    b, length, _ = x.shape
    n_tiles = length // tm
    halo_per_tile = tm // HALO
    n_halo_blocks = length // HALO
    weight = functools.partial(_layer_block, layer=layer, pipeline_mode=pl.Buffered(1))
    in_specs = [
        pl.BlockSpec((1, tm, D_MODEL), lambda bi, i: (bi, i, 0)),
        pl.BlockSpec((1, tm, V_COLS), lambda bi, i: (bi, i, 0)),
        pl.BlockSpec((1, HALO, CP_COLS), lambda bi, i: (bi, jnp.maximum(i * halo_per_tile - 1, 0), 0)),
        pl.BlockSpec((1, tm, CP_COLS), lambda bi, i: (bi, i, 0)),
        pl.BlockSpec((1, HALO, CP_COLS),
                     lambda bi, i: (bi, jnp.minimum((i + 1) * halo_per_tile, n_halo_blocks - 1), 0)),
        pl.BlockSpec((tm, POOL_WIDTH), lambda bi, i: (i, 0)),
    ] + [_mod_block(layer, which, mod_row) for which in (2, 3, 4, 5)] + [
        _layer_block((1, D_MODEL), layer),
        _layer_block((3, CONV_WIDTH), layer),
        weight((POOL_WIDTH, POOL_WIDTH)),
        _layer_block((1, POOL_WIDTH), layer),
        weight((D_MODEL, D_MODEL)),
        weight((D_MODEL, 2 * FFN_HIDDEN)),
        weight((FFN_HIDDEN, D_MODEL)),
        pl.BlockSpec((1, D_MODEL), lambda bi, i: (0, 0)),
    ]
    return pl.pallas_call(
        functools.partial(_mixffn_kernel, length=length, tm=tm, final=final),
        grid=(b, n_tiles),
        in_specs=in_specs,
        out_specs=pl.BlockSpec((1, tm, D_MODEL), lambda bi, i: (bi, i, 0)),
        out_shape=jax.ShapeDtypeStruct((b, length, D_MODEL), F32),
        scratch_shapes=[pltpu.VMEM((tm, D_MODEL), F32)],
        compiler_params=_params(),
        name="mix_ffn_lat" if mod_row is None else "mix_ffn_ctx",
    )(x, o, cp, cp, cp, _pool_inv_counts(length), mod, mod, mod, mod, g_norm2, w_conv, w_pool_bd, s_pool, w_out,
      w_gate_up, w_down, g_final)


@jax.jit
def _forward(x, c, ctx, c_ctx, w_ada, b_ada, g_norm1, w_in, lam_q1, lam_k1, lam_q2, lam_k2,
             g_subln, w_conv, w_pool, s_pool, w_out, g_norm2, w_gate_up, w_down, g_final):
    batch, seq_len, _ = x.shape
    ctx_len = ctx.shape[1]
    ctx_row = batch
    assert batch < MOD_ROWS
    tiles = _choose_tiles(seq_len, ctx_len)

    cc = jnp.concatenate([c, c_ctx[None, :], jnp.zeros((MOD_ROWS - batch - 1, D_MODEL), F32)], axis=0)
    mod = _ada_rows(cc, w_ada, b_ada).reshape(DEPTH, MOD_ROWS, N_MOD, 1, D_MODEL)

    w_in_b = w_in.astype(BF16)
    w_out_b = w_out.astype(BF16)
    w_gu_b = w_gate_up.astype(BF16)
    w_down_b = w_down.astype(BF16)
    eye = jnp.eye(POOL_GROUPS, dtype=F32)
    w_pool_bd = (w_pool[:, :, :, None, :] * eye[None, :, None, :, None]).reshape(
        DEPTH, POOL_WIDTH, POOL_WIDTH).astype(BF16)
    lam_rows = jnp.stack([lam_q1, lam_k1, lam_q2, lam_k2], axis=1)
    g1 = g_norm1[:, None, :]
    g2 = g_norm2[:, None, :]
    gs = g_subln[:, None, :]
    sp = s_pool[:, None, :]
    gf = g_final[None, :]
    tables = _rope_tables(seq_len)

    x_lat, x_ctx = x, ctx
    for l in range(DEPTH):
        last = l == DEPTH - 1
        lam_init = 0.8 - 0.6 * math.exp(-0.3 * l)
        ctx_proj = _inproj(x_ctx, mod, ctx_row, g1, w_in_b, None, layer=l, tm=tiles.tm_ctx, kv_only=last)
        q_l, k_l, vt_l, cp_l = _inproj(x_lat, mod, None, g1, w_in_b, tables, layer=l, tm=tiles.tm_in)
        k_c, vt_c = ctx_proj[-3:-1] if not last else ctx_proj
        o_l = _attention(lam_rows, gs, q_l, [k_c, k_l], [vt_c, vt_l], layer=l, lam_init=lam_init,
                         tq=tiles.tq_lat, heads=1)
        mix_args = (g2, w_conv, w_pool_bd, sp, w_out_b, w_gu_b, w_down_b, gf)
        x_lat = _mixffn(x_lat, o_l, cp_l, mod, None, *mix_args, layer=l, tm=tiles.tm_lat, final=last)
        if not last:
            q_c, _, _, cp_c = ctx_proj
            o_c = _attention(lam_rows, gs, q_c, [k_c], [vt_c], layer=l, lam_init=lam_init,
                             tq=tiles.tq_ctx, heads=ATTN_HEADS)
            x_ctx = _mixffn(x_ctx, o_c, cp_c, mod, ctx_row, *mix_args, layer=l, tm=tiles.tm_ctx, final=False)
    return x_lat


def kernel(x, c, ctx, c_ctx, w_ada, b_ada, g_norm1, w_in, lam_q1, lam_k1, lam_q2, lam_k2, g_subln, w_conv,
           w_pool, s_pool, w_out, g_norm2, w_gate_up, w_down, g_final):
    return _forward(x, c, ctx, c_ctx, w_ada, b_ada, g_norm1, w_in, lam_q1, lam_k1, lam_q2, lam_k2, g_subln,
                    w_conv, w_pool, s_pool, w_out, g_norm2, w_gate_up, w_down, g_final)
```

```python
import functools
import math
from typing import NamedTuple

import jax
import jax.numpy as jnp
from jax import lax
from jax.experimental import pallas as pl
from jax.experimental.pallas import tpu as pltpu

D_MODEL = 1024
DEPTH = 4
GRID_W = 64
N_MOD = 6
ATTN_HEADS = 4
ATTN_QK_DIM = 64
ATTN_V_DIM = 2 * ATTN_QK_DIM
HEAD_COLS = 2 * ATTN_QK_DIM
Q_COLS = ATTN_HEADS * HEAD_COLS
V_COLS = ATTN_HEADS * ATTN_V_DIM
ATTN_SCALE = ATTN_QK_DIM ** -0.5
ROPE_THETA = 10000.0
ROPE_HALF = ATTN_QK_DIM // 4
CONV_WIDTH = D_MODEL // 4
POOL_WINDOWS = (2, 4, 8, 16)
POOL_GROUPS = len(POOL_WINDOWS)
POOL_WIDTH = D_MODEL - V_COLS - CONV_WIDTH
POOL_GROUP_DIM = POOL_WIDTH // POOL_GROUPS
CP_COLS = 3 * CONV_WIDTH + POOL_WIDTH
IN_COLS = 3 * Q_COLS + CP_COLS
FFN_HIDDEN = -(-(8 * D_MODEL) // (3 * 256)) * 256
EPS = 1e-6

V7X_LANES = 128
V7X_MXU_COLS = 256
V7X_BF16_SUBLANES = 16
V7X_VMEM_LIMIT_BYTES = 56 * 1024 * 1024

MOD_ROWS = 40
ONES_ROWS = V7X_BF16_SUBLANES
KV_SETS = 3
HALO = V7X_BF16_SUBLANES
FFN_CHUNK = V7X_MXU_COLS
N_FFN_CHUNKS = FFN_HIDDEN // FFN_CHUNK
ADA_COL_BLOCK = 1536
Q_PRESCALE = ATTN_SCALE * math.log2(math.e)

F32 = jnp.float32
BF16 = jnp.bfloat16


class Tiles(NamedTuple):
    tm_in: int
    tm_lat: int
    tm_ctx: int
    tq_lat: int
    tq_ctx: int


def _choose_tiles(seq_len, ctx_len):
    return Tiles(tm_in=min(1024, seq_len), tm_lat=min(512, seq_len), tm_ctx=min(512, ctx_len),
                 tq_lat=min(V7X_MXU_COLS, seq_len), tq_ctx=min(V7X_MXU_COLS, ctx_len))


def _params(grid_rank=2):
    return pltpu.CompilerParams(dimension_semantics=("arbitrary",) * grid_rank,
                                vmem_limit_bytes=V7X_VMEM_LIMIT_BYTES)


def _dot(a, b):
    return jnp.dot(a, b, preferred_element_type=F32)


def _rms(x):
    return x * lax.rsqrt(jnp.mean(x * x, axis=-1, keepdims=True) + EPS)


def _layer_block(shape, layer, **kwargs):
    return pl.BlockSpec((None,) + tuple(shape), lambda bi, i: (layer,) + (0,) * len(shape), **kwargs)


def _mod_block(layer, which, mod_row):
    row = (lambda bi: bi) if mod_row is None else (lambda bi: mod_row)
    return pl.BlockSpec((None, None, None, 1, D_MODEL), lambda bi, i: (layer, row(bi), which, 0, 0))


def _ada_kernel(cc_ref, w_ref, b_ref, out_ref):
    cc = cc_ref[...]
    s = cc * jax.nn.sigmoid(cc)
    out_ref[0] = _dot(s.astype(BF16), w_ref[0].astype(BF16)) + b_ref[0]


def _ada_rows(cc, w_ada, b_ada):
    n_col = N_MOD * D_MODEL
    return pl.pallas_call(
        _ada_kernel,
        grid=(DEPTH, n_col // ADA_COL_BLOCK),
        in_specs=[
            pl.BlockSpec((MOD_ROWS, D_MODEL), lambda l, j: (0, 0)),
            pl.BlockSpec((1, D_MODEL, ADA_COL_BLOCK), lambda l, j: (l, 0, j)),
            pl.BlockSpec((1, 1, ADA_COL_BLOCK), lambda l, j: (l, 0, j)),
        ],
        out_specs=pl.BlockSpec((1, MOD_ROWS, ADA_COL_BLOCK), lambda l, j: (l, 0, j)),
        out_shape=jax.ShapeDtypeStruct((DEPTH, MOD_ROWS, n_col), F32),
        compiler_params=_params(),
        name="ada_rows",
    )(cc, w_ada, b_ada.reshape(DEPTH, 1, n_col))


def _rope_tables(length):
    pos = jnp.arange(length)
    row = (pos // GRID_W).astype(F32)
    col = (pos % GRID_W).astype(F32)
    inv = 1.0 / (ROPE_THETA ** (jnp.arange(ROPE_HALF, dtype=F32) / ROPE_HALF))
    ang_r = row[:, None] * inv[None, :]
    ang_c = col[:, None] * inv[None, :]
    ang = jnp.concatenate([ang_r, ang_r, ang_c, ang_c], axis=-1)
    cos = jnp.tile(jnp.cos(ang).astype(F32), (1, HEAD_COLS // ATTN_QK_DIM))
    sin = jnp.tile(jnp.sin(ang).astype(F32), (1, HEAD_COLS // ATTN_QK_DIM))
    first_half = (jnp.arange(HEAD_COLS) % (2 * ROPE_HALF)) < ROPE_HALF
    sin_from_upper = jnp.where(first_half, -sin, 0.0)
    sin_from_lower = jnp.where(first_half, 0.0, sin)
    return cos, sin_from_upper, sin_from_lower


def _pool_inv_counts(length):
    t = jnp.arange(length)[:, None]
    half = jnp.repeat(jnp.array(POOL_WINDOWS, dtype=jnp.int32) // 2, POOL_GROUP_DIM)[None, :]
    cnt = jnp.minimum(t + half - 1, length - 1) - jnp.maximum(t - half, 0) + 1
    return 1.0 / cnt.astype(F32)


def _mixers(cp, inv_cnt_ref, wconv_ref, wpool_ref, spool_ref, *, tm, length):
    i = pl.program_id(1)
    n_tiles = length // tm
    rows = tm + 2 * HALO
    core = lambda a: a[HALO:HALO + tm]

    def shifted(a, d):
        return pltpu.roll(a, d % rows, 0)

    cur = core(cp)
    ext = jnp.concatenate([jnp.where(i > 0, cp[0:HALO], 0.0), cur,
                           jnp.where(i < n_tiles - 1, cp[HALO + tm:rows], 0.0)], axis=0)

    u = ext[:, CONV_WIDTH:2 * CONV_WIDTH] * ext[:, 2 * CONV_WIDTH:3 * CONV_WIDTH]
    wc = wconv_ref[...]
    y = shifted(u, 1) * wc[0:1] + u * wc[1:2] + shifted(u, -1) * wc[2:3]
    conv = cur[:, 0:CONV_WIDTH] * core(y)

    lane = lax.broadcasted_iota(jnp.int32, (tm, V7X_LANES), 1)
    pooled = []
    for tile in range(POOL_WIDTH // V7X_LANES):
        lanes = slice(tile * V7X_LANES, (tile + 1) * V7X_LANES)
        p = ext[:, 3 * CONV_WIDTH + tile * V7X_LANES:3 * CONV_WIDTH + (tile + 1) * V7X_LANES]
        sums = {2: p + shifted(p, 1)}
        for w in (4, 8, 16):
            sums[w] = shifted(sums[w // 2], w // 4) + shifted(sums[w // 2], -(w // 4))
        w_lo, w_hi = POOL_WINDOWS[2 * tile], POOL_WINDOWS[2 * tile + 1]
        win_sum = jnp.where(lane < POOL_GROUP_DIM, core(sums[w_lo]), core(sums[w_hi]))
        pooled.append(win_sum * inv_cnt_ref[:, lanes] - core(p))
    pooled = jnp.concatenate(pooled, axis=1).astype(BF16)
    pool_y = _dot(pooled, wpool_ref[...]) * spool_ref[...]
    return jnp.concatenate([conv.astype(BF16), pool_y.astype(BF16)], axis=1)


def _inproj_kernel(*refs, rope, kv_only, tm, length):
    x_ref, sh_ref, sc_ref, g_ref, w_ref = refs[:5]
    refs = refs[5:]
    if rope:
        cos_ref, sup_ref, slo_ref = refs[:3]
        refs = refs[3:]
    if kv_only:
        k_ref, vt_ref = refs
        x = x_ref[0]
    else:
        (x_prev_ref, x_next_ref, inv_cnt_ref, wconv_ref, wpool_ref, spool_ref,
         q_ref, k_ref, vt_ref, mix_ref) = refs
        x = jnp.concatenate([x_prev_ref[0], x_ref[0], x_next_ref[0]], axis=0)

    h = _rms(x) * (g_ref[...] * (1.0 + sc_ref[...])) + sh_ref[...]
    hb = h.astype(BF16)
    if not kv_only:
        cp = _dot(hb, w_ref[:, 3 * Q_COLS:IN_COLS])
        hb = hb[HALO:HALO + tm]

    def rotate(t, scale):
        if not rope:
            return t if scale == 1.0 else t * scale
        out = []
        for j in range(Q_COLS // HEAD_COLS):
            tj = t[:, j * HEAD_COLS:(j + 1) * HEAD_COLS]
            rj = (tj * cos_ref[...]
                  + pltpu.roll(tj, HEAD_COLS - ROPE_HALF, 1) * sup_ref[...]
                  + pltpu.roll(tj, ROPE_HALF, 1) * slo_ref[...])
            out.append(rj if scale == 1.0 else rj * scale)
        return jnp.concatenate(out, axis=1)

    if not kv_only:
        q = _dot(hb, w_ref[:, 0:Q_COLS])
        q_ref[0] = rotate(q, Q_PRESCALE).astype(BF16)
    k = _dot(hb, w_ref[:, Q_COLS:2 * Q_COLS])
    k_ref[0] = rotate(k, 1.0).astype(BF16)
    if not kv_only:
        mix_ref[0] = _mixers(cp, inv_cnt_ref, wconv_ref, wpool_ref, spool_ref, tm=tm, length=length)
    v = _dot(hb, w_ref[:, 2 * Q_COLS:3 * Q_COLS])
    vt_ref[0] = v.T.astype(BF16)


def _inproj(x, mod, mod_row, g_norm1, w_in, tables, mixer_params, *, layer, tm, kv_only=False):
    b, length, _ = x.shape
    rope = tables is not None
    halo_per_tile = tm // HALO
    n_halo_blocks = length // HALO
    in_specs = [
        pl.BlockSpec((1, tm, D_MODEL), lambda bi, i: (bi, i, 0)),
        _mod_block(layer, 0, mod_row),
        _mod_block(layer, 1, mod_row),
        _layer_block((1, D_MODEL), layer),
        _layer_block((D_MODEL, IN_COLS), layer, pipeline_mode=pl.Buffered(1)),
    ]
    args = [x, mod, mod, g_norm1, w_in]
    if rope:
        in_specs += [pl.BlockSpec((tm, HEAD_COLS), lambda bi, i: (i, 0))] * 3
        args += list(tables)
    row_spec = lambda cols: pl.BlockSpec((1, tm, cols), lambda bi, i: (bi, i, 0))
    row_shape = lambda cols: jax.ShapeDtypeStruct((b, length, cols), BF16)
    vt_spec = pl.BlockSpec((1, V_COLS, tm), lambda bi, i: (bi, 0, i))
    vt_shape = jax.ShapeDtypeStruct((b, V_COLS, length), BF16)
    if kv_only:
        out_specs = [row_spec(Q_COLS), vt_spec]
        out_shape = [row_shape(Q_COLS), vt_shape]
    else:
        w_conv, w_pool_bd, s_pool = mixer_params
        in_specs += [
            pl.BlockSpec((1, HALO, D_MODEL), lambda bi, i: (bi, jnp.maximum(i * halo_per_tile - 1, 0), 0)),
            pl.BlockSpec((1, HALO, D_MODEL),
                         lambda bi, i: (bi, jnp.minimum((i + 1) * halo_per_tile, n_halo_blocks - 1), 0)),
            pl.BlockSpec((tm, POOL_WIDTH), lambda bi, i: (i, 0)),
            _layer_block((3, CONV_WIDTH), layer),
            _layer_block((POOL_WIDTH, POOL_WIDTH), layer),
            _layer_block((1, POOL_WIDTH), layer),
        ]
        args += [x, x, _pool_inv_counts(length), w_conv, w_pool_bd, s_pool]
        out_specs = [row_spec(Q_COLS), row_spec(Q_COLS), vt_spec, row_spec(CONV_WIDTH + POOL_WIDTH)]
        out_shape = [row_shape(Q_COLS), row_shape(Q_COLS), vt_shape, row_shape(CONV_WIDTH + POOL_WIDTH)]
    return pl.pallas_call(
        functools.partial(_inproj_kernel, rope=rope, kv_only=kv_only, tm=tm, length=length),
        grid=(b, length // tm),
        in_specs=in_specs,
        out_specs=out_specs,
        out_shape=out_shape,
        compiler_params=_params(),
        name="inproj_lat" if rope else "inproj_ctx",
    )(*args)


def _attn_kernel(*refs, key_lens, lam_init, tq, heads):
    n_src = len(key_lens)
    lam_ref, g_ref, q_ref = refs[:3]
    k_refs = refs[3:3 + n_src]
    vt_refs = refs[3 + n_src:3 + 2 * n_src]
    o_ref = refs[3 + 2 * n_src]
    scratch = refs[4 + 2 * n_src:]
    k_all, vt_all = scratch[0:KV_SETS], scratch[KV_SETS:2 * KV_SETS]
    scratch = scratch[2 * KV_SETS:]
    s_ref = (scratch[0:2], scratch[2:4])
    p_ref = (scratch[4:6], scratch[6:8])

    lq = lam_ref[...]
    lam = (jnp.exp(jnp.sum(lq[0:1] * lq[1:2], axis=1, keepdims=True))
           - jnp.exp(jnp.sum(lq[2:3] * lq[3:4], axis=1, keepdims=True)) + lam_init)
    lane = lax.broadcasted_iota(jnp.int32, (tq, HEAD_COLS), 1)
    n_q = q_ref.shape[1] // tq
    lk_total = sum(key_lens)
    tiles = [(head, i) for head in range(heads) for i in range(n_q)]
    head_cols = lambda head: slice(head * HEAD_COLS, (head + 1) * HEAD_COLS)
    q_rows = lambda i: slice(i * tq, (i + 1) * tq)

    def load_head(head):
        off = 0
        for k_ref, vt_ref, lk in zip(k_refs, vt_refs, key_lens):
            k_all[head % KV_SETS][off:off + lk, :] = k_ref[0, :, head_cols(head)]
            vt_all[head % KV_SETS][0:ATTN_V_DIM, off:off + lk] = vt_ref[0, head_cols(head), :]
            off += lk
        vt_all[head % KV_SETS][ATTN_V_DIM:, :] = jnp.ones((ONES_ROWS, lk_total), BF16)

    def scores(t):
        head, i = tiles[t]
        if i == 0:
            load_head(head)
        q = q_ref[0, q_rows(i), head_cols(head)]
        zero = jnp.zeros_like(q)
        maxima = []
        for c, keep in enumerate((lane < ATTN_QK_DIM, lane >= ATTN_QK_DIM)):
            s = lax.dot_general(k_all[head % KV_SETS][...], jnp.where(keep, q, zero),
                                (((1,), (1,)), ((), ())), preferred_element_type=F32)
            s_ref[t % 2][c][...] = s
            maxima.append(jnp.max(s, axis=0, keepdims=True))
        return tuple(maxima)

    def softmax(t, maxima):
        for c in range(2):
            p_ref[t % 2][c][...] = jnp.exp2(s_ref[t % 2][c][...] - maxima[c]).astype(BF16)

    def values(t):
        head, i = tiles[t]
        outs = []
        for c in range(2):
            av = _dot(vt_all[head % KV_SETS][...], p_ref[t % 2][c][...])
            outs.append(av[0:ATTN_V_DIM] / av[ATTN_V_DIM:ATTN_V_DIM + 1])
        o_t = outs[0] - lam * outs[1]
        y = o_t * lax.rsqrt(jnp.mean(o_t * o_t, axis=0, keepdims=True) + EPS)
        o = (y.T * g_ref[...]) * (1.0 - lam_init)
        o_ref[0, q_rows(i), head_cols(head)] = o.astype(BF16)

    n = len(tiles)
    maxima = {0: scores(0)}
    if n > 1:
        maxima[1] = scores(1)
    softmax(0, maxima[0])
    for t in range(n):
        if t + 2 < n:
            maxima[t + 2] = scores(t + 2)
        if t + 1 < n:
            softmax(t + 1, maxima[t + 1])
        values(t)


def _attention(lam_rows, g_subln, q, ks, vts, *, layer, lam_init, tq, heads):
    b, lq, _ = q.shape
    key_lens = tuple(k.shape[1] for k in ks)
    lk_total = sum(key_lens)
    cols = heads * HEAD_COLS
    head_rows = lambda length: pl.BlockSpec((1, length, cols), lambda bi, h: (bi, 0, h))
    in_specs = [
        _layer_block((4, ATTN_QK_DIM), layer),
        _layer_block((1, ATTN_V_DIM), layer),
        head_rows(lq),
    ]
    in_specs += [head_rows(lk) for lk in key_lens]
    in_specs += [pl.BlockSpec((1, cols, lk), lambda bi, h: (bi, h, 0)) for lk in key_lens]
    return pl.pallas_call(
        functools.partial(_attn_kernel, key_lens=key_lens, lam_init=lam_init, tq=tq, heads=heads),
        grid=(b, ATTN_HEADS // heads),
        in_specs=in_specs,
        out_specs=head_rows(lq),
        out_shape=jax.ShapeDtypeStruct((b, lq, V_COLS), BF16),
        scratch_shapes=[*[pltpu.VMEM((lk_total, HEAD_COLS), BF16)] * KV_SETS,
                        *[pltpu.VMEM((ATTN_V_DIM + ONES_ROWS, lk_total), BF16)] * KV_SETS,
                        *[pltpu.VMEM((lk_total, tq), F32)] * 4, *[pltpu.VMEM((lk_total, tq), BF16)] * 4],
        compiler_params=_params(),
        name="diff_attn_lat" if len(ks) == 2 else "diff_attn_ctx",
    )(lam_rows, g_subln, q, *ks, *vts)


def _mixffn_kernel(x_ref, o_ref, mix_ref, gt1_ref, sh2_ref, sc2_ref, gt2_ref, g2_ref, wout_ref, wgu_ref, wd_ref,
                   gfin_ref, out_ref, acc_ref, *, final):
    mixed = jnp.concatenate([o_ref[0], mix_ref[0]], axis=1)
    x1 = x_ref[0] + gt1_ref[...] * _dot(mixed, wout_ref[...])
    out_ref[0] = x1
    h2 = (_rms(x1) * (g2_ref[...] * (1.0 + sc2_ref[...])) + sh2_ref[...]).astype(BF16)

    for c in range(N_FFN_CHUNKS):
        gate = _dot(h2, wgu_ref[:, c * FFN_CHUNK:(c + 1) * FFN_CHUNK])
        up = _dot(h2, wgu_ref[:, FFN_HIDDEN + c * FFN_CHUNK:FFN_HIDDEN + (c + 1) * FFN_CHUNK])
        act = (gate * jax.nn.sigmoid(gate) * up).astype(BF16)
        part = _dot(act, wd_ref[c * FFN_CHUNK:(c + 1) * FFN_CHUNK, :])
        if c == 0:
            acc_ref[...] = part
        else:
            acc_ref[...] += part

    x2 = out_ref[0] + gt2_ref[...] * acc_ref[...]
    if final:
        x2 = _rms(x2) * gfin_ref[...]
    out_ref[0] = x2


def _mixffn(x, o, mix, mod, mod_row, g_norm2, w_out, w_gate_up, w_down, g_final, *, layer, tm, final):
    b, length, _ = x.shape
    weight = functools.partial(_layer_block, layer=layer, pipeline_mode=pl.Buffered(1))
    rows = lambda cols: pl.BlockSpec((1, tm, cols), lambda bi, i: (bi, i, 0))
    in_specs = [rows(D_MODEL), rows(V_COLS), rows(CONV_WIDTH + POOL_WIDTH)]
    in_specs += [_mod_block(layer, which, mod_row) for which in (2, 3, 4, 5)]
    in_specs += [
        _layer_block((1, D_MODEL), layer),
        weight((D_MODEL, D_MODEL)),
        weight((D_MODEL, 2 * FFN_HIDDEN)),
        weight((FFN_HIDDEN, D_MODEL)),
        pl.BlockSpec((1, D_MODEL), lambda bi, i: (0, 0)),
    ]
    return pl.pallas_call(
        functools.partial(_mixffn_kernel, final=final),
        grid=(b, length // tm),
        in_specs=in_specs,
        out_specs=rows(D_MODEL),
        out_shape=jax.ShapeDtypeStruct((b, length, D_MODEL), F32),
        scratch_shapes=[pltpu.VMEM((tm, D_MODEL), F32)],
        compiler_params=_params(),
        name="mix_ffn_lat" if mod_row is None else "mix_ffn_ctx",
    )(x, o, mix, mod, mod, mod, mod, g_norm2, w_out, w_gate_up, w_down, g_final)


@jax.jit
def _forward(x, c, ctx, c_ctx, w_ada, b_ada, g_norm1, w_in, lam_q1, lam_k1, lam_q2, lam_k2,
             g_subln, w_conv, w_pool, s_pool, w_out, g_norm2, w_gate_up, w_down, g_final):
    batch, seq_len, _ = x.shape
    ctx_len = ctx.shape[1]
    ctx_row = batch
    assert batch < MOD_ROWS
    tiles = _choose_tiles(seq_len, ctx_len)

    cc = jnp.concatenate([c, c_ctx[None, :], jnp.zeros((MOD_ROWS - batch - 1, D_MODEL), F32)], axis=0)
    mod = _ada_rows(cc, w_ada, b_ada).reshape(DEPTH, MOD_ROWS, N_MOD, 1, D_MODEL)

    w_in_b = w_in.astype(BF16)
    w_out_b = w_out.astype(BF16)
    w_gu_b = w_gate_up.astype(BF16)
    w_down_b = w_down.astype(BF16)
    eye = jnp.eye(POOL_GROUPS, dtype=F32)
    w_pool_bd = (w_pool[:, :, :, None, :] * eye[None, :, None, :, None]).reshape(
        DEPTH, POOL_WIDTH, POOL_WIDTH).astype(BF16)
    lam_rows = jnp.stack([lam_q1, lam_k1, lam_q2, lam_k2], axis=1)
    g1 = g_norm1[:, None, :]
    g2 = g_norm2[:, None, :]
    gs = g_subln[:, None, :]
    sp = s_pool[:, None, :]
    gf = g_final[None, :]
    tables = _rope_tables(seq_len)

    x_lat, x_ctx = x, ctx
    for l in range(DEPTH):
        last = l == DEPTH - 1
        lam_init = 0.8 - 0.6 * math.exp(-0.3 * l)
        mixer_params = (w_conv, w_pool_bd, sp)
        ctx_proj = _inproj(x_ctx, mod, ctx_row, g1, w_in_b, None, mixer_params, layer=l, tm=tiles.tm_ctx,
                           kv_only=last)
        q_l, k_l, vt_l, mix_l = _inproj(x_lat, mod, None, g1, w_in_b, tables, mixer_params, layer=l,
                                        tm=tiles.tm_in)
        k_c, vt_c = ctx_proj[-3:-1] if not last else ctx_proj
        o_l = _attention(lam_rows, gs, q_l, [k_c, k_l], [vt_c, vt_l], layer=l, lam_init=lam_init,
                         tq=tiles.tq_lat, heads=1)
        ffn_args = (g2, w_out_b, w_gu_b, w_down_b, gf)
        x_lat = _mixffn(x_lat, o_l, mix_l, mod, None, *ffn_args, layer=l, tm=tiles.tm_lat, final=last)
        if not last:
            q_c, _, _, mix_c = ctx_proj
            o_c = _attention(lam_rows, gs, q_c, [k_c], [vt_c], layer=l, lam_init=lam_init,
                             tq=tiles.tq_ctx, heads=ATTN_HEADS)
            x_ctx = _mixffn(x_ctx, o_c, mix_c, mod, ctx_row, *ffn_args, layer=l, tm=tiles.tm_ctx, final=False)
    return x_lat


def kernel(x, c, ctx, c_ctx, w_ada, b_ada, g_norm1, w_in, lam_q1, lam_k1, lam_q2, lam_k2, g_subln, w_conv,
           w_pool, s_pool, w_out, g_norm2, w_gate_up, w_down, g_final):
    return _forward(x, c, ctx, c_ctx, w_ada, b_ada, g_norm1, w_in, lam_q1, lam_k1, lam_q2, lam_k2, g_subln,
                    w_conv, w_pool, s_pool, w_out, g_norm2, w_gate_up, w_down, g_final)
```

```python
import functools
import math
from typing import NamedTuple

import jax
import jax.numpy as jnp
from jax import lax
from jax.experimental import pallas as pl
from jax.experimental.pallas import tpu as pltpu

D_MODEL = 1024
DEPTH = 4
GRID_W = 64
N_MOD = 6
ATTN_HEADS = 4
ATTN_QK_DIM = 64
ATTN_V_DIM = 2 * ATTN_QK_DIM
HEAD_COLS = 2 * ATTN_QK_DIM
Q_COLS = ATTN_HEADS * HEAD_COLS
V_COLS = ATTN_HEADS * ATTN_V_DIM
ATTN_SCALE = ATTN_QK_DIM ** -0.5
ROPE_THETA = 10000.0
ROPE_HALF = ATTN_QK_DIM // 4
CONV_WIDTH = D_MODEL // 4
POOL_WINDOWS = (2, 4, 8, 16)
POOL_GROUPS = len(POOL_WINDOWS)
POOL_WIDTH = D_MODEL - V_COLS - CONV_WIDTH
POOL_GROUP_DIM = POOL_WIDTH // POOL_GROUPS
CP_COLS = 3 * CONV_WIDTH + POOL_WIDTH
IN_COLS = 3 * Q_COLS + CP_COLS
FFN_HIDDEN = -(-(8 * D_MODEL) // (3 * 256)) * 256
EPS = 1e-6

V7X_LANES = 128
V7X_MXU_COLS = 256
V7X_BF16_SUBLANES = 16
V7X_VMEM_LIMIT_BYTES = 56 * 1024 * 1024

MOD_ROWS = 40
ONES_ROWS = V7X_BF16_SUBLANES
KV_SETS = 3
HALO = V7X_BF16_SUBLANES
NORM_ROWS = V7X_MXU_COLS
FFN_CHUNK = V7X_MXU_COLS
N_FFN_CHUNKS = FFN_HIDDEN // FFN_CHUNK
ADA_COL_BLOCK = 1536
Q_PRESCALE = ATTN_SCALE * math.log2(math.e)

F32 = jnp.float32
BF16 = jnp.bfloat16


class Tiles(NamedTuple):
    tm_in: int
    tm_lat: int
    tm_ctx: int
    tq_lat: int
    tq_ctx: int


def _choose_tiles(seq_len, ctx_len):
    return Tiles(tm_in=min(1024, seq_len), tm_lat=min(512, seq_len), tm_ctx=min(512, ctx_len),
                 tq_lat=min(V7X_MXU_COLS, seq_len), tq_ctx=min(V7X_MXU_COLS, ctx_len))


def _params(grid_rank=2):
    return pltpu.CompilerParams(dimension_semantics=("arbitrary",) * grid_rank,
                                vmem_limit_bytes=V7X_VMEM_LIMIT_BYTES)


def _dot(a, b):
    return jnp.dot(a, b, preferred_element_type=F32)


def _rms(x):
    return x * lax.rsqrt(jnp.mean(x * x, axis=-1, keepdims=True) + EPS)


def _layer_block(shape, layer, **kwargs):
    return pl.BlockSpec((None,) + tuple(shape), lambda bi, i: (layer,) + (0,) * len(shape), **kwargs)


def _mod_block(layer, which, mod_row):
    row = (lambda bi: bi) if mod_row is None else (lambda bi: mod_row)
    return pl.BlockSpec((None, None, None, 1, D_MODEL), lambda bi, i: (layer, row(bi), which, 0, 0))


def _ada_kernel(cc_ref, w_ref, b_ref, out_ref):
    cc = cc_ref[...]
    s = cc * jax.nn.sigmoid(cc)
    out_ref[0] = _dot(s.astype(BF16), w_ref[0].astype(BF16)) + b_ref[0]


def _ada_rows(cc, w_ada, b_ada):
    n_col = N_MOD * D_MODEL
    return pl.pallas_call(
        _ada_kernel,
        grid=(DEPTH, n_col // ADA_COL_BLOCK),
        in_specs=[
            pl.BlockSpec((MOD_ROWS, D_MODEL), lambda l, j: (0, 0)),
            pl.BlockSpec((1, D_MODEL, ADA_COL_BLOCK), lambda l, j: (l, 0, j)),
            pl.BlockSpec((1, 1, ADA_COL_BLOCK), lambda l, j: (l, 0, j)),
        ],
        out_specs=pl.BlockSpec((1, MOD_ROWS, ADA_COL_BLOCK), lambda l, j: (l, 0, j)),
        out_shape=jax.ShapeDtypeStruct((DEPTH, MOD_ROWS, n_col), F32),
        compiler_params=_params(),
        name="ada_rows",
    )(cc, w_ada, b_ada.reshape(DEPTH, 1, n_col))


def _rope_tables(length):
    pos = jnp.arange(length)
    row = (pos // GRID_W).astype(F32)
    col = (pos % GRID_W).astype(F32)
    inv = 1.0 / (ROPE_THETA ** (jnp.arange(ROPE_HALF, dtype=F32) / ROPE_HALF))
    ang_r = row[:, None] * inv[None, :]
    ang_c = col[:, None] * inv[None, :]
    ang = jnp.concatenate([ang_r, ang_r, ang_c, ang_c], axis=-1)
    cos = jnp.tile(jnp.cos(ang).astype(F32), (1, HEAD_COLS // ATTN_QK_DIM))
    sin = jnp.tile(jnp.sin(ang).astype(F32), (1, HEAD_COLS // ATTN_QK_DIM))
    first_half = (jnp.arange(HEAD_COLS) % (2 * ROPE_HALF)) < ROPE_HALF
    sin_from_upper = jnp.where(first_half, -sin, 0.0)
    sin_from_lower = jnp.where(first_half, 0.0, sin)
    return cos, sin_from_upper, sin_from_lower


def _pool_inv_counts(length):
    t = jnp.arange(length)[:, None]
    half = jnp.repeat(jnp.array(POOL_WINDOWS, dtype=jnp.int32) // 2, POOL_GROUP_DIM)[None, :]
    cnt = jnp.minimum(t + half - 1, length - 1) - jnp.maximum(t - half, 0) + 1
    return 1.0 / cnt.astype(F32)


def _mixers(cp, inv_cnt_ref, wconv_ref, wpool_ref, spool_ref, *, tm, length):
    i = pl.program_id(1)
    n_tiles = length // tm
    rows = tm + 2 * HALO
    core = lambda a: a[HALO:HALO + tm]

    def shifted(a, d):
        return pltpu.roll(a, d % rows, 0)

    cur = core(cp)
    ext = jnp.concatenate([jnp.where(i > 0, cp[0:HALO], 0.0), cur,
                           jnp.where(i < n_tiles - 1, cp[HALO + tm:rows], 0.0)], axis=0)

    u = ext[:, CONV_WIDTH:2 * CONV_WIDTH] * ext[:, 2 * CONV_WIDTH:3 * CONV_WIDTH]
    wc = wconv_ref[...]
    y = shifted(u, 1) * wc[0:1] + u * wc[1:2] + shifted(u, -1) * wc[2:3]
    conv = cur[:, 0:CONV_WIDTH] * core(y)

    lane = lax.broadcasted_iota(jnp.int32, (tm, V7X_LANES), 1)
    pooled = []
    for tile in range(POOL_WIDTH // V7X_LANES):
        lanes = slice(tile * V7X_LANES, (tile + 1) * V7X_LANES)
        p = ext[:, 3 * CONV_WIDTH + tile * V7X_LANES:3 * CONV_WIDTH + (tile + 1) * V7X_LANES]
        sums = {2: p + shifted(p, 1)}
        for w in (4, 8, 16):
            sums[w] = shifted(sums[w // 2], w // 4) + shifted(sums[w // 2], -(w // 4))
        w_lo, w_hi = POOL_WINDOWS[2 * tile], POOL_WINDOWS[2 * tile + 1]
        win_sum = jnp.where(lane < POOL_GROUP_DIM, core(sums[w_lo]), core(sums[w_hi]))
        pooled.append(win_sum * inv_cnt_ref[:, lanes] - core(p))
    pooled = jnp.concatenate(pooled, axis=1).astype(BF16)
    pool_y = _dot(pooled, wpool_ref[...]) * spool_ref[...]
    return jnp.concatenate([conv.astype(BF16), pool_y.astype(BF16)], axis=1)


def _inproj_kernel(*refs, rope, kv_only, tm, length):
    x_ref, sh_ref, sc_ref, g_ref, w_ref = refs[:5]
    refs = refs[5:]
    if rope:
        cos_ref, sup_ref, slo_ref = refs[:3]
        refs = refs[3:]
    if kv_only:
        k_ref, vt_ref = refs
        x = x_ref[0]
    else:
        (x_prev_ref, x_next_ref, inv_cnt_ref, wconv_ref, wpool_ref, spool_ref,
         q_ref, k_ref, vt_ref, mix_ref) = refs
        x = jnp.concatenate([x_prev_ref[0], x_ref[0], x_next_ref[0]], axis=0)

    h = _rms(x) * (g_ref[...] * (1.0 + sc_ref[...])) + sh_ref[...]
    hb = h.astype(BF16)
    if not kv_only:
        cp = _dot(hb, w_ref[:, 3 * Q_COLS:IN_COLS])
        hb = hb[HALO:HALO + tm]

    def rotate(t, scale):
        if not rope:
            return t if scale == 1.0 else t * scale
        out = []
        for j in range(Q_COLS // HEAD_COLS):
            tj = t[:, j * HEAD_COLS:(j + 1) * HEAD_COLS]
            rj = (tj * cos_ref[...]
                  + pltpu.roll(tj, HEAD_COLS - ROPE_HALF, 1) * sup_ref[...]
                  + pltpu.roll(tj, ROPE_HALF, 1) * slo_ref[...])
            out.append(rj if scale == 1.0 else rj * scale)
        return jnp.concatenate(out, axis=1)

    if not kv_only:
        q = _dot(hb, w_ref[:, 0:Q_COLS])
        q_ref[0] = rotate(q, Q_PRESCALE).astype(BF16)
    k = _dot(hb, w_ref[:, Q_COLS:2 * Q_COLS])
    k_ref[0] = rotate(k, 1.0).astype(BF16)
    if not kv_only:
        mix_ref[0] = _mixers(cp, inv_cnt_ref, wconv_ref, wpool_ref, spool_ref, tm=tm, length=length)
    v = _dot(hb, w_ref[:, 2 * Q_COLS:3 * Q_COLS])
    vt_ref[0] = v.T.astype(BF16)


def _inproj(x, mod, mod_row, g_norm1, w_in, tables, mixer_params, *, layer, tm, kv_only=False):
    b, length, _ = x.shape
    rope = tables is not None
    halo_per_tile = tm // HALO
    n_halo_blocks = length // HALO
    in_specs = [
        pl.BlockSpec((1, tm, D_MODEL), lambda bi, i: (bi, i, 0)),
        _mod_block(layer, 0, mod_row),
        _mod_block(layer, 1, mod_row),
        _layer_block((1, D_MODEL), layer),
        _layer_block((D_MODEL, IN_COLS), layer, pipeline_mode=pl.Buffered(1)),
    ]
    args = [x, mod, mod, g_norm1, w_in]
    if rope:
        in_specs += [pl.BlockSpec((tm, HEAD_COLS), lambda bi, i: (i, 0))] * 3
        args += list(tables)
    row_spec = lambda cols: pl.BlockSpec((1, tm, cols), lambda bi, i: (bi, i, 0))
    row_shape = lambda cols: jax.ShapeDtypeStruct((b, length, cols), BF16)
    vt_spec = pl.BlockSpec((1, V_COLS, tm), lambda bi, i: (bi, 0, i))
    vt_shape = jax.ShapeDtypeStruct((b, V_COLS, length), BF16)
    if kv_only:
        out_specs = [row_spec(Q_COLS), vt_spec]
        out_shape = [row_shape(Q_COLS), vt_shape]
    else:
        w_conv, w_pool_bd, s_pool = mixer_params
        in_specs += [
            pl.BlockSpec((1, HALO, D_MODEL), lambda bi, i: (bi, jnp.maximum(i * halo_per_tile - 1, 0), 0)),
            pl.BlockSpec((1, HALO, D_MODEL),
                         lambda bi, i: (bi, jnp.minimum((i + 1) * halo_per_tile, n_halo_blocks - 1), 0)),
            pl.BlockSpec((tm, POOL_WIDTH), lambda bi, i: (i, 0)),
            _layer_block((3, CONV_WIDTH), layer),
            _layer_block((POOL_WIDTH, POOL_WIDTH), layer),
            _layer_block((1, POOL_WIDTH), layer),
        ]
        args += [x, x, _pool_inv_counts(length), w_conv, w_pool_bd, s_pool]
        out_specs = [row_spec(Q_COLS), row_spec(Q_COLS), vt_spec, row_spec(CONV_WIDTH + POOL_WIDTH)]
        out_shape = [row_shape(Q_COLS), row_shape(Q_COLS), vt_shape, row_shape(CONV_WIDTH + POOL_WIDTH)]
    return pl.pallas_call(
        functools.partial(_inproj_kernel, rope=rope, kv_only=kv_only, tm=tm, length=length),
        grid=(b, length // tm),
        in_specs=in_specs,
        out_specs=out_specs,
        out_shape=out_shape,
        compiler_params=_params(),
        name="inproj_lat" if rope else "inproj_ctx",
    )(*args)


def _attn_kernel(*refs, key_lens, lam_init, tq, heads):
    n_src = len(key_lens)
    lam_ref, g_ref, q_ref = refs[:3]
    k_refs = refs[3:3 + n_src]
    vt_refs = refs[3 + n_src:3 + 2 * n_src]
    o_ref = refs[3 + 2 * n_src]
    scratch = refs[4 + 2 * n_src:]
    k_all, vt_all = scratch[0:KV_SETS], scratch[KV_SETS:2 * KV_SETS]
    scratch = scratch[2 * KV_SETS:]
    s_ref = (scratch[0:2], scratch[2:4])
    p_ref = (scratch[4:6], scratch[6:8])

    lq = lam_ref[...]
    lam = (jnp.exp(jnp.sum(lq[0:1] * lq[1:2], axis=1, keepdims=True))
           - jnp.exp(jnp.sum(lq[2:3] * lq[3:4], axis=1, keepdims=True)) + lam_init)
    lane = lax.broadcasted_iota(jnp.int32, (tq, HEAD_COLS), 1)
    n_q = q_ref.shape[1] // tq
    lk_total = sum(key_lens)
    tiles = [(head, i) for head in range(heads) for i in range(n_q)]
    head_cols = lambda head: slice(head * HEAD_COLS, (head + 1) * HEAD_COLS)
    q_rows = lambda i: slice(i * tq, (i + 1) * tq)

    def load_head(head):
        off = 0
        for k_ref, vt_ref, lk in zip(k_refs, vt_refs, key_lens):
            k_all[head % KV_SETS][off:off + lk, :] = k_ref[0, :, head_cols(head)]
            vt_all[head % KV_SETS][0:ATTN_V_DIM, off:off + lk] = vt_ref[0, head_cols(head), :]
            off += lk
        vt_all[head % KV_SETS][ATTN_V_DIM:, :] = jnp.ones((ONES_ROWS, lk_total), BF16)

    def scores(t):
        head, i = tiles[t]
        if i == 0:
            load_head(head)
        q = q_ref[0, q_rows(i), head_cols(head)]
        zero = jnp.zeros_like(q)
        maxima = []
        for c, keep in enumerate((lane < ATTN_QK_DIM, lane >= ATTN_QK_DIM)):
            s = lax.dot_general(k_all[head % KV_SETS][...], jnp.where(keep, q, zero),
                                (((1,), (1,)), ((), ())), preferred_element_type=F32)
            s_ref[t % 2][c][...] = s
            maxima.append(jnp.max(s, axis=0, keepdims=True))
        return tuple(maxima)

    def softmax(t, maxima):
        for c in range(2):
            p_ref[t % 2][c][...] = jnp.exp2(s_ref[t % 2][c][...] - maxima[c]).astype(BF16)

    def values(t):
        head, i = tiles[t]
        outs = []
        for c in range(2):
            av = _dot(vt_all[head % KV_SETS][...], p_ref[t % 2][c][...])
            outs.append(av[0:ATTN_V_DIM] / av[ATTN_V_DIM:ATTN_V_DIM + 1])
        o_t = outs[0] - lam * outs[1]
        y = o_t * lax.rsqrt(jnp.mean(o_t * o_t, axis=0, keepdims=True) + EPS)
        o = (y.T * g_ref[...]) * (1.0 - lam_init)
        o_ref[0, q_rows(i), head_cols(head)] = o.astype(BF16)

    n = len(tiles)
    maxima = {0: scores(0)}
    if n > 1:
        maxima[1] = scores(1)
    softmax(0, maxima[0])
    for t in range(n):
        if t + 2 < n:
            maxima[t + 2] = scores(t + 2)
        if t + 1 < n:
            softmax(t + 1, maxima[t + 1])
        values(t)


def _attention(lam_rows, g_subln, q, ks, vts, *, layer, lam_init, tq, heads):
    b, lq, _ = q.shape
    key_lens = tuple(k.shape[1] for k in ks)
    lk_total = sum(key_lens)
    cols = heads * HEAD_COLS
    head_rows = lambda length: pl.BlockSpec((1, length, cols), lambda bi, h: (bi, 0, h))
    in_specs = [
        _layer_block((4, ATTN_QK_DIM), layer),
        _layer_block((1, ATTN_V_DIM), layer),
        head_rows(lq),
    ]
    in_specs += [head_rows(lk) for lk in key_lens]
    in_specs += [pl.BlockSpec((1, cols, lk), lambda bi, h: (bi, h, 0)) for lk in key_lens]
    return pl.pallas_call(
        functools.partial(_attn_kernel, key_lens=key_lens, lam_init=lam_init, tq=tq, heads=heads),
        grid=(b, ATTN_HEADS // heads),
        in_specs=in_specs,
        out_specs=head_rows(lq),
        out_shape=jax.ShapeDtypeStruct((b, lq, V_COLS), BF16),
        scratch_shapes=[*[pltpu.VMEM((lk_total, HEAD_COLS), BF16)] * KV_SETS,
                        *[pltpu.VMEM((ATTN_V_DIM + ONES_ROWS, lk_total), BF16)] * KV_SETS,
                        *[pltpu.VMEM((lk_total, tq), F32)] * 4, *[pltpu.VMEM((lk_total, tq), BF16)] * 4],
        compiler_params=_params(),
        name="diff_attn_lat" if len(ks) == 2 else "diff_attn_ctx",
    )(lam_rows, g_subln, q, *ks, *vts)


def _mixffn_kernel(x_ref, o_ref, mix_ref, gt1_ref, sh2_ref, sc2_ref, gt2_ref, g2_ref, wout_ref, wgu_ref, wd_ref,
                   gfin_ref, out_ref, acc_ref, *, final):
    tm = x_ref.shape[1]
    gain2 = g2_ref[...] * (1.0 + sc2_ref[...])
    h2 = []
    for r0 in range(0, tm, NORM_ROWS):
        rows = slice(r0, r0 + NORM_ROWS)
        mixed = jnp.concatenate([o_ref[0, rows, :], mix_ref[0, rows, :]], axis=1)
        x1 = x_ref[0, rows, :] + gt1_ref[...] * _dot(mixed, wout_ref[...])
        out_ref[0, rows, :] = x1
        h2.append((_rms(x1) * gain2 + sh2_ref[...]).astype(BF16))
    h2 = jnp.concatenate(h2, axis=0)

    for c in range(N_FFN_CHUNKS):
        gate = _dot(h2, wgu_ref[:, c * FFN_CHUNK:(c + 1) * FFN_CHUNK])
        up = _dot(h2, wgu_ref[:, FFN_HIDDEN + c * FFN_CHUNK:FFN_HIDDEN + (c + 1) * FFN_CHUNK])
        act = (gate * jax.nn.sigmoid(gate) * up).astype(BF16)
        part = _dot(act, wd_ref[c * FFN_CHUNK:(c + 1) * FFN_CHUNK, :])
        if c == 0:
            acc_ref[...] = part
        else:
            acc_ref[...] += part

    x2 = out_ref[0] + gt2_ref[...] * acc_ref[...]
    if final:
        x2 = _rms(x2) * gfin_ref[...]
    out_ref[0] = x2


def _mixffn(x, o, mix, mod, mod_row, g_norm2, w_out, w_gate_up, w_down, g_final, *, layer, tm, final):
    b, length, _ = x.shape
    weight = functools.partial(_layer_block, layer=layer, pipeline_mode=pl.Buffered(1))
    rows = lambda cols: pl.BlockSpec((1, tm, cols), lambda bi, i: (bi, i, 0))
    in_specs = [rows(D_MODEL), rows(V_COLS), rows(CONV_WIDTH + POOL_WIDTH)]
    in_specs += [_mod_block(layer, which, mod_row) for which in (2, 3, 4, 5)]
    in_specs += [
        _layer_block((1, D_MODEL), layer),
        weight((D_MODEL, D_MODEL)),
        weight((D_MODEL, 2 * FFN_HIDDEN)),
        weight((FFN_HIDDEN, D_MODEL)),
        pl.BlockSpec((1, D_MODEL), lambda bi, i: (0, 0)),
    ]
    return pl.pallas_call(
        functools.partial(_mixffn_kernel, final=final),
        grid=(b, length // tm),
        in_specs=in_specs,
        out_specs=rows(D_MODEL),
        out_shape=jax.ShapeDtypeStruct((b, length, D_MODEL), F32),
        scratch_shapes=[pltpu.VMEM((tm, D_MODEL), F32)],
        compiler_params=_params(),
        name="mix_ffn_lat" if mod_row is None else "mix_ffn_ctx",
    )(x, o, mix, mod, mod, mod, mod, g_norm2, w_out, w_gate_up, w_down, g_final)


@jax.jit
def _forward(x, c, ctx, c_ctx, w_ada, b_ada, g_norm1, w_in, lam_q1, lam_k1, lam_q2, lam_k2,
             g_subln, w_conv, w_pool, s_pool, w_out, g_norm2, w_gate_up, w_down, g_final):
    batch, seq_len, _ = x.shape
    ctx_len = ctx.shape[1]
    ctx_row = batch
    assert batch < MOD_ROWS
    tiles = _choose_tiles(seq_len, ctx_len)

    cc = jnp.concatenate([c, c_ctx[None, :], jnp.zeros((MOD_ROWS - batch - 1, D_MODEL), F32)], axis=0)
    mod = _ada_rows(cc, w_ada, b_ada).reshape(DEPTH, MOD_ROWS, N_MOD, 1, D_MODEL)

    w_in_b = w_in.astype(BF16)
    w_out_b = w_out.astype(BF16)
    w_gu_b = w_gate_up.astype(BF16)
    w_down_b = w_down.astype(BF16)
    eye = jnp.eye(POOL_GROUPS, dtype=F32)
    w_pool_bd = (w_pool[:, :, :, None, :] * eye[None, :, None, :, None]).reshape(
        DEPTH, POOL_WIDTH, POOL_WIDTH).astype(BF16)
    lam_rows = jnp.stack([lam_q1, lam_k1, lam_q2, lam_k2], axis=1)
    g1 = g_norm1[:, None, :]
    g2 = g_norm2[:, None, :]
    gs = g_subln[:, None, :]
    sp = s_pool[:, None, :]
    gf = g_final[None, :]
    tables = _rope_tables(seq_len)

    x_lat, x_ctx = x, ctx
    for l in range(DEPTH):
        last = l == DEPTH - 1
        lam_init = 0.8 - 0.6 * math.exp(-0.3 * l)
        mixer_params = (w_conv, w_pool_bd, sp)
        ctx_proj = _inproj(x_ctx, mod, ctx_row, g1, w_in_b, None, mixer_params, layer=l, tm=tiles.tm_ctx,
                           kv_only=last)
        q_l, k_l, vt_l, mix_l = _inproj(x_lat, mod, None, g1, w_in_b, tables, mixer_params, layer=l,
                                        tm=tiles.tm_in)
        k_c, vt_c = ctx_proj[-3:-1] if not last else ctx_proj
        o_l = _attention(lam_rows, gs, q_l, [k_c, k_l], [vt_c, vt_l], layer=l, lam_init=lam_init,
                         tq=tiles.tq_lat, heads=1)
        ffn_args = (g2, w_out_b, w_gu_b, w_down_b, gf)
        x_lat = _mixffn(x_lat, o_l, mix_l, mod, None, *ffn_args, layer=l, tm=tiles.tm_lat, final=last)
        if not last:
            q_c, _, _, mix_c = ctx_proj
            o_c = _attention(lam_rows, gs, q_c, [k_c], [vt_c], layer=l, lam_init=lam_init,
                             tq=tiles.tq_ctx, heads=ATTN_HEADS)
            x_ctx = _mixffn(x_ctx, o_c, mix_c, mod, ctx_row, *ffn_args, layer=l, tm=tiles.tm_ctx, final=False)
    return x_lat


def kernel(x, c, ctx, c_ctx, w_ada, b_ada, g_norm1, w_in, lam_q1, lam_k1, lam_q2, lam_k2, g_subln, w_conv,
           w_pool, s_pool, w_out, g_norm2, w_gate_up, w_down, g_final):
    return _forward(x, c, ctx, c_ctx, w_ada, b_ada, g_norm1, w_in, lam_q1, lam_k1, lam_q2, lam_k2, g_subln,
                    w_conv, w_pool, s_pool, w_out, g_norm2, w_gate_up, w_down, g_final)
```
